```python
import math
import jax
import jax.numpy as jnp
from jax import lax
import numpy as np

D_MODEL = 2048
BATCH = 2
SEQ = 8192
DEPTH = 2

CTX_LEN = 256
GRID_W = 64
NA_HEADS = 6
NA_HEAD_DIM = 128
NA_WIN_H = 8
NA_WIN_W = 16
RET_HEADS = 4
RET_QK_DIM = 128
RET_V_DIM = 256
RET_CHUNK = 128
DIFF_HEADS = 6
DIFF_QK_DIM = 64
DIFF_V_DIM = 128
DIFF_Q_BLOCK = 128
ROPE_BASE = 10000.0
FFN_DIM = 5632
N_EXPERTS = 8
TOP_K = 2
EXPERT_DIM = 7168
N_BRANCHES = 3
NORM_EPS = 1e-6
SUBLN_EPS = 1e-5
NEG_INF = -1e30

NA_W = NA_HEADS * NA_HEAD_DIM
RET_QK_W = RET_HEADS * RET_QK_DIM
RET_V_W = RET_HEADS * RET_V_DIM
DIFF_QK_W = DIFF_HEADS * 2 * DIFF_QK_DIM
DIFF_V_W = DIFF_HEADS * DIFF_V_DIM
IN_SPLIT_WIDTHS = (NA_W, NA_W, NA_W, RET_QK_W, RET_QK_W, RET_V_W, RET_V_W, DIFF_QK_W, DIFF_QK_W, DIFF_V_W, N_BRANCHES * D_MODEL)
IN_W = sum(IN_SPLIT_WIDTHS)
IN_SPLIT_POINTS = tuple(sum(IN_SPLIT_WIDTHS[:i + 1]) for i in range(len(IN_SPLIT_WIDTHS) - 1))

kernel_name = "hybrid_natten_retention_diffattn_moe_dit"


def _rms_norm(x, g, eps=NORM_EPS):
    xf = x.astype(jnp.float32)
    y = xf * lax.rsqrt(jnp.mean(xf * xf, axis=-1, keepdims=True) + eps)
    return (y * g.astype(jnp.float32)).astype(x.dtype)


def _adaln(cond, w_ada, b_ada):
    m = jax.nn.silu(cond) @ w_ada + b_ada
    chunks = jnp.split(m, 6, axis=-1)
    if cond.ndim == 2:
        chunks = [t[:, None, :] for t in chunks]
    return chunks


def _modulate(xn, shift, scale):
    return xn * (1.0 + scale) + shift


def _heads(t, n):
    b, s, w = t.shape
    return t.reshape(b, s, n, w // n).transpose(0, 2, 1, 3)


def _merge_heads(t):
    b, h, s, d = t.shape
    return t.transpose(0, 2, 1, 3).reshape(b, s, h * d)


def _diff_heads(t):
    b, s, _ = t.shape
    return t.reshape(b, s, DIFF_HEADS, 2, DIFF_QK_DIM).transpose(0, 2, 3, 1, 4)


def _axial_rope(n_tokens, dim):
    t = jnp.arange(n_tokens)
    row = (t // GRID_W).astype(jnp.float32)
    col = (t % GRID_W).astype(jnp.float32)
    axis_dim = dim // 2
    inv_freq = ROPE_BASE ** (-jnp.arange(0, axis_dim, 2, dtype=jnp.float32) / axis_dim)
    ar = row[:, None] * inv_freq
    ac = col[:, None] * inv_freq
    ang = jnp.concatenate([ar, ar, ac, ac], axis=-1)
    return jnp.cos(ang), jnp.sin(ang)


def _rotate_axial_half(t):
    ts = t.reshape(t.shape[:-1] + (2, 2, t.shape[-1] // 4))
    t1 = ts[..., 0, :]
    t2 = ts[..., 1, :]
    return jnp.stack([-t2, t1], axis=-2).reshape(t.shape)


def _apply_rope(t, cos, sin):
    tf = t.astype(jnp.float32)
    return (tf * cos + _rotate_axial_half(tf) * sin).astype(t.dtype)


def _softmax_attend(q, k, v):
    s = jnp.einsum('bhqd,bhkd->bhqk', q, k).astype(jnp.float32) * (q.shape[-1] ** -0.5)
    p = jax.nn.softmax(s, axis=-1)
    return jnp.einsum('bhqk,bhkd->bhqd', p.astype(v.dtype), v)


def _neighbourhood_attention(q, k, v, kc, vc, rpb):
    b, h, s, dh = q.shape
    rows = s // GRID_W
    kh = min(NA_WIN_H, rows)
    scale = dh ** -0.5
    qg = q.reshape(b, h, rows, GRID_W, dh)
    kg = k.reshape(b, h, rows, GRID_W, dh)
    vg = v.reshape(b, h, rows, GRID_W, dh)
    qrow = jnp.arange(rows)
    r0 = jnp.clip(qrow - kh // 2, 0, rows - kh)
    key_rows = r0[:, None] + jnp.arange(kh)[None, :]
    kb = kg[:, :, key_rows]
    vb = vg[:, :, key_rows]
    cols = jnp.arange(GRID_W)
    c0 = jnp.clip(cols - NA_WIN_W // 2, 0, GRID_W - NA_WIN_W)
    col_in = (cols[None, :] >= c0[:, None]) & (cols[None, :] < c0[:, None] + NA_WIN_W)
    dr_idx = key_rows - qrow[:, None] + (NA_WIN_H - 1)
    dc_idx = jnp.clip(cols[None, :] - cols[:, None] + (NA_WIN_W - 1), 0, 2 * NA_WIN_W - 2)
    bias = rpb[:, dr_idx[:, None, :, None], dc_idx[None, :, None, :]].astype(jnp.float32)
    s_lat = jnp.einsum('bhrqd,bhrkwd->bhrqkw', qg, kb).astype(jnp.float32) * scale + bias[None]
    s_lat = jnp.where(col_in[:, None, :], s_lat, NEG_INF)
    s_ctx = jnp.einsum('bhrqd,bhld->bhrql', qg, kc).astype(jnp.float32) * scale
    n_lat = kh * GRID_W
    scores = jnp.concatenate([s_lat.reshape(b, h, rows, GRID_W, n_lat), s_ctx], axis=-1)
    p = jax.nn.softmax(scores, axis=-1).astype(v.dtype)
    p_lat = p[..., :n_lat].reshape(b, h, rows, GRID_W, kh, GRID_W)
    o = (jnp.einsum('bhrqkw,bhrkwd->bhrqd', p_lat, vb)
         + jnp.einsum('bhrql,bhld->bhrqd', p[..., n_lat:], vc))
    return o.reshape(b, h, s, dh)


def _retention_chunkwise(q, k, v, log_gamma, state):
    b, h, n, _ = q.shape
    dv = v.shape[-1]
    cs = RET_CHUNK
    nc = n // cs
    pos = jnp.arange(cs, dtype=jnp.float32)
    lg = log_gamma[:, None]
    q_decay = jnp.exp(lg * (pos + 1.0))[None, :, :, None]
    k_decay = jnp.exp(lg * (cs - 1.0 - pos))[None, :, :, None]
    rel = pos[:, None] - pos[None, :]
    intra = jnp.where(rel >= 0, jnp.exp(lg[:, :, None] * jnp.maximum(rel, 0.0)), 0.0)[None]
    chunk_decay = jnp.exp(lg * cs)[None, :, :, None]

    def to_chunks(t):
        return t.astype(jnp.float32).reshape(b, h, nc, cs, t.shape[-1]).transpose(2, 0, 1, 3, 4)

    def step(s_prev, inp):
        qi, ki, vi = inp
        att = jnp.einsum('bhnd,bhmd->bhnm', qi, ki) * intra
        o = (jnp.einsum('bhnm,bhmv->bhnv', att, vi)
             + jnp.einsum('bhnd,bhdv->bhnv', qi * q_decay, s_prev))
        s_new = s_prev * chunk_decay + jnp.einsum('bhmd,bhmv->bhdv', ki * k_decay, vi)
        return s_new, o

    state, o = lax.scan(step, state, (to_chunks(q), to_chunks(k), to_chunks(v)))
    return o.transpose(1, 2, 0, 3, 4).reshape(b, h, n, dv), state


def _bidirectional_retention(q, k, v, qc, kc, vc, lg_f, lg_b):
    b, h, _, dk = q.shape
    dv = v.shape[-1]
    s0 = jnp.zeros((b, h, dk, dv), jnp.float32)

    def rev(t):
        return jnp.flip(t, axis=2)

    oc_f, sc_f = _retention_chunkwise(qc, kc, vc, lg_f, s0)
    oc_b, sc_b = _retention_chunkwise(rev(qc), rev(kc), rev(vc), lg_b, s0)
    o_f, _ = _retention_chunkwise(q, k, v, lg_f, sc_f)
    o_b, _ = _retention_chunkwise(rev(q), rev(k), rev(v), lg_b, sc_b)
    return o_f + rev(o_b), oc_f + rev(oc_b)


def _ret_head_norm(o, gain):
    of = o.astype(jnp.float32)
    mu = jnp.mean(of, axis=-1, keepdims=True)
    var = jnp.mean(jnp.square(of - mu), axis=-1, keepdims=True)
    y = (of - mu) * lax.rsqrt(var + NORM_EPS)
    return _merge_heads(y) * gain.astype(jnp.float32)


def _diff_attend(q, k, v, lam):
    s = jnp.einsum('bhmqd,bhmkd->bhmqk', q, k).astype(jnp.float32) * (DIFF_QK_DIM ** -0.5)
    p = jax.nn.softmax(s, axis=-1)
    a = p[:, :, 0] - lam * p[:, :, 1]
    return jnp.einsum('bhqk,bhkd->bhqd', a.astype(v.dtype), v)


def _diff_attention_latent(q, k, v, kc, vc, lam):
    b, h, _, s, dq = q.shape
    k_all = jnp.concatenate([k, kc], axis=3)
    v_all = jnp.concatenate([v, vc], axis=2)
    nb = s // DIFF_Q_BLOCK
    qb = q.reshape(b, h, 2, nb, DIFF_Q_BLOCK, dq).transpose(3, 0, 1, 2, 4, 5)
    o = lax.map(lambda qi: _diff_attend(qi, k_all, v_all, lam), qb)
    return o.transpose(1, 2, 0, 3, 4).reshape(b, h, s, v.shape[-1])


def _diff_head_norm(o, gain, lam_init):
    of = o.astype(jnp.float32)
    y = of * lax.rsqrt(jnp.mean(of * of, axis=-1, keepdims=True) + SUBLN_EPS)
    return _merge_heads(y * gain.astype(jnp.float32) * (1.0 - lam_init))


def _merge_branches(oa, ob, od, gates, p):
    g = jax.nn.sigmoid(gates.reshape(gates.shape[:-1] + (N_BRANCHES, D_MODEL)))
    y = (g[..., 0, :] * (oa @ p['w_br_a'])
         + g[..., 1, :] * (ob @ p['w_br_b'])
         + g[..., 2, :] * (od @ p['w_br_c']))
    return y @ p['w_out']


def _token_mixers(u, uc, p, layer_idx, cos, sin, need_ctx):
    qa, ka, va, qb, kb, vb, gb, qd, kd, vd, gates = jnp.split(u @ p['w_in'], IN_SPLIT_POINTS, axis=-1)
    qac, kac, vac, qbc, kbc, vbc, gbc, qdc, kdc, vdc, gatesc = jnp.split(uc @ p['w_in'], IN_SPLIT_POINTS, axis=-1)

    kac_h = _heads(kac, NA_HEADS)
    vac_h = _heads(vac, NA_HEADS)
    oa = _neighbourhood_attention(_heads(qa, NA_HEADS), _heads(ka, NA_HEADS), _heads(va, NA_HEADS),
                                  kac_h, vac_h, p['na_rpb'])

    k_scale = RET_QK_DIM ** -0.5
    log_gamma = jax.nn.log_sigmoid(p['ret_decay'].astype(jnp.float32))
    ob, obc = _bidirectional_retention(
        _heads(qb, RET_HEADS), _heads(kb, RET_HEADS) * k_scale, _heads(vb, RET_HEADS),
        _heads(qbc, RET_HEADS), _heads(kbc, RET_HEADS) * k_scale, _heads(vbc, RET_HEADS),
        log_gamma[0], log_gamma[1])
    ob = jax.nn.silu(gb) * _ret_head_norm(ob, p['ret_gn'])

    lam_init = 0.8 - 0.6 * math.exp(-0.3 * layer_idx)
    lq1, lk1, lq2, lk2 = p['diff_lam'].astype(jnp.float32)
    lam = jnp.exp(jnp.sum(lq1 * lk1)) - jnp.exp(jnp.sum(lq2 * lk2)) + lam_init
    kdc_h = _diff_heads(kdc)
    vdc_h = _heads(vdc, DIFF_HEADS)
    od = _diff_attention_latent(_apply_rope(_diff_heads(qd), cos, sin), _apply_rope(_diff_heads(kd), cos, sin),
                                _heads(vd, DIFF_HEADS), kdc_h, vdc_h, lam)
    od = _diff_head_norm(od, p['diff_subln'], lam_init)

    y = _merge_branches(_merge_heads(oa), ob, od, gates, p)
    if not need_ctx:
        return y, None
    oac = _softmax_attend(_heads(qac, NA_HEADS), kac_h, vac_h)
    obc = jax.nn.silu(gbc) * _ret_head_norm(obc, p['ret_gn'])
    odc = _diff_head_norm(_diff_attend(_diff_heads(qdc), kdc_h, vdc_h, lam), p['diff_subln'], lam_init)
    yc = _merge_branches(_merge_heads(oac), obc, odc, gatesc, p)
    return y, yc


def _swiglu(u, w1, w3, w2):
    return (jax.nn.silu(u @ w1) * (u @ w3)) @ w2


def _moe_swiglu(u, router_w, router_b, w1, w3, w2):
    logits = (u @ router_w).astype(jnp.float32) + router_b.astype(jnp.float32)
    top_val, top_idx = lax.top_k(logits, TOP_K)
    top_w = jax.nn.softmax(top_val, axis=-1)
    combine = jnp.sum(jax.nn.one_hot(top_idx, N_EXPERTS, dtype=jnp.float32) * top_w[..., None], axis=-2)
    out = jnp.zeros(u.shape, jnp.float32)
    for e in range(N_EXPERTS):
        out = out + combine[..., e:e + 1] * _swiglu(u, w1[e], w3[e], w2[e])
    return out.astype(u.dtype)


def _channel_mixer(u, p, layer_idx):
    if layer_idx % 2 == 0:
        return _swiglu(u, p['ffn_w1'], p['ffn_w3'], p['ffn_w2'])
    return _moe_swiglu(u, p['router_w'], p['router_b'], p['exp_w1'], p['exp_w3'], p['exp_w2'])


def _layer(h, hc, c, c_ctx, p, layer_idx, cos, sin, need_ctx):
    sh1, sc1, g1, sh2, sc2, g2 = _adaln(c, p['w_ada'], p['b_ada'])
    csh1, csc1, cg1, csh2, csc2, cg2 = _adaln(c_ctx, p['w_ada'], p['b_ada'])
    u = _modulate(_rms_norm(h, p['norm1']), sh1, sc1)
    uc = _modulate(_rms_norm(hc, p['norm1']), csh1, csc1)
    y, yc = _token_mixers(u, uc, p, layer_idx, cos, sin, need_ctx)
    h = h + g1 * y
    u2 = _modulate(_rms_norm(h, p['norm2']), sh2, sc2)
    h = h + g2 * _channel_mixer(u2, p, layer_idx)
    if need_ctx:
        hc = hc + cg1 * yc
        uc2 = _modulate(_rms_norm(hc, p['norm2']), csh2, csc2)
        hc = hc + cg2 * _channel_mixer(uc2, p, layer_idx)
    return h, hc


def _layer_params(key, layer_idx):
    ks = jax.random.split(key, 20)
    d = D_MODEL
    f32 = jnp.float32

    def nrm(k, shape, s):
        return jax.random.normal(k, shape, f32) * s

    gamma0 = 1.0 - 2.0 ** (-5.0 - jnp.arange(RET_HEADS, dtype=f32))
    decay_logit = jnp.log(gamma0) - jnp.log1p(-gamma0)
    params = [
        ('w_ada', nrm(ks[0], (d, 6 * d), 0.5 * d ** -0.5)),
        ('b_ada', nrm(ks[1], (6 * d,), 0.02)),
        ('norm1', 1.0 + nrm(ks[2], (d,), 0.02)),
        ('w_in', nrm(ks[3], (d, IN_W), d ** -0.5)),
        ('na_rpb', nrm(ks[4], (NA_HEADS, 2 * NA_WIN_H - 1, 2 * NA_WIN_W - 1), 0.1)),
        ('ret_decay', decay_logit[None, :] + nrm(ks[5], (2, RET_HEADS), 0.05)),
        ('ret_gn', 1.0 + nrm(ks[6], (RET_V_W,), 0.02)),
        ('diff_lam', nrm(ks[7], (4, DIFF_QK_DIM), 0.1)),
        ('diff_subln', 1.0 + nrm(ks[8], (DIFF_V_DIM,), 0.02)),
        ('w_br_a', nrm(ks[9], (NA_W, d), NA_W ** -0.5)),
        ('w_br_b', nrm(ks[10], (RET_V_W, d), RET_V_W ** -0.5)),
        ('w_br_c', nrm(ks[11], (DIFF_V_W, d), DIFF_V_W ** -0.5)),
        ('w_out', nrm(ks[12], (d, d), d ** -0.5)),
        ('norm2', 1.0 + nrm(ks[13], (d,), 0.02)),
    ]
    if layer_idx % 2 == 0:
        params += [
            ('ffn_w1', nrm(ks[14], (d, FFN_DIM), d ** -0.5)),
            ('ffn_w3', nrm(ks[15], (d, FFN_DIM), d ** -0.5)),
            ('ffn_w2', nrm(ks[16], (FFN_DIM, d), FFN_DIM ** -0.5)),
        ]
    else:
        params += [
            ('router_w', nrm(ks[14], (d, N_EXPERTS), d ** -0.5)),
            ('router_b', nrm(ks[15], (N_EXPERTS,), 0.01)),
            ('exp_w1', nrm(ks[16], (N_EXPERTS, d, EXPERT_DIM), d ** -0.5)),
            ('exp_w3', nrm(ks[17], (N_EXPERTS, d, EXPERT_DIM), d ** -0.5)),
            ('exp_w2', nrm(ks[18], (N_EXPERTS, EXPERT_DIM, d), EXPERT_DIM ** -0.5)),
        ]
    return params


def setup_inputs(seed: int = 0) -> dict:
    key = jax.random.key(seed)
    k_x, k_c, k_ctx, k_cc, k_fn, k_layers = jax.random.split(key, 6)
    inputs = {
        'x': jax.random.normal(k_x, (BATCH, SEQ, D_MODEL), jnp.float32),
        'c': jax.random.normal(k_c, (BATCH, D_MODEL), jnp.float32),
        'ctx': jax.random.normal(k_ctx, (BATCH, CTX_LEN, D_MODEL), jnp.float32),
        'c_ctx': jax.random.normal(k_cc, (D_MODEL,), jnp.float32),
    }
    layer_keys = jax.random.split(k_layers, DEPTH)
    for i in range(DEPTH):
        for name, arr in _layer_params(layer_keys[i], i):
            inputs['l%d_%s' % (i, name)] = arr
    inputs['final_norm'] = 1.0 + 0.02 * jax.random.normal(k_fn, (D_MODEL,), jnp.float32)
    return inputs


def reference(x, c, ctx, c_ctx,
              l0_w_ada, l0_b_ada, l0_norm1, l0_w_in, l0_na_rpb, l0_ret_decay, l0_ret_gn, l0_diff_lam,
              l0_diff_subln, l0_w_br_a, l0_w_br_b, l0_w_br_c, l0_w_out, l0_norm2,
              l0_ffn_w1, l0_ffn_w3, l0_ffn_w2,
              l1_w_ada, l1_b_ada, l1_norm1, l1_w_in, l1_na_rpb, l1_ret_decay, l1_ret_gn, l1_diff_lam,
              l1_diff_subln, l1_w_br_a, l1_w_br_b, l1_w_br_c, l1_w_out, l1_norm2,
              l1_router_w, l1_router_b, l1_exp_w1, l1_exp_w3, l1_exp_w2,
              final_norm):
    layers = (
        dict(w_ada=l0_w_ada, b_ada=l0_b_ada, norm1=l0_norm1, w_in=l0_w_in, na_rpb=l0_na_rpb,
             ret_decay=l0_ret_decay, ret_gn=l0_ret_gn, diff_lam=l0_diff_lam, diff_subln=l0_diff_subln,
             w_br_a=l0_w_br_a, w_br_b=l0_w_br_b, w_br_c=l0_w_br_c, w_out=l0_w_out, norm2=l0_norm2,
             ffn_w1=l0_ffn_w1, ffn_w3=l0_ffn_w3, ffn_w2=l0_ffn_w2),
        dict(w_ada=l1_w_ada, b_ada=l1_b_ada, norm1=l1_norm1, w_in=l1_w_in, na_rpb=l1_na_rpb,
             ret_decay=l1_ret_decay, ret_gn=l1_ret_gn, diff_lam=l1_diff_lam, diff_subln=l1_diff_subln,
             w_br_a=l1_w_br_a, w_br_b=l1_w_br_b, w_br_c=l1_w_br_c, w_out=l1_w_out, norm2=l1_norm2,
             router_w=l1_router_w, router_b=l1_router_b, exp_w1=l1_exp_w1, exp_w3=l1_exp_w3,
             exp_w2=l1_exp_w2),
    )
    n_tokens = x.shape[1]
    cos, sin = _axial_rope(n_tokens, DIFF_QK_DIM)
    h, hc = x, ctx
    for i in range(DEPTH):
        h, hc = _layer(h, hc, c, c_ctx, layers[i], i, cos, sin, i < DEPTH - 1)
    return _rms_norm(h, final_norm)
```

```python
import functools
import math

import jax
import jax.numpy as jnp
from jax import lax
from jax.experimental import pallas as pl
from jax.experimental.pallas import tpu as pltpu

D_MODEL = 2048
GRID_W = 64
NA_HEADS = 6
NA_HEAD_DIM = 128
NA_WIN_H = 8
NA_WIN_W = 16
RET_HEADS = 4
RET_QK_DIM = 128
RET_V_DIM = 256
RET_CHUNK = 128
DIFF_HEADS = 6
DIFF_QK_DIM = 64
DIFF_V_DIM = 128
ROPE_BASE = 10000.0
N_EXPERTS = 8
TOP_K = 2
N_BRANCHES = 3
NORM_EPS = 1e-6
SUBLN_EPS = 1e-5
NEG_INF = -1e30

LANES = 128
ROW_TILE = 512
MOE_ROW_TILE = 512
VMEM_LIMIT = 56 << 20

F32 = jnp.float32
BF16 = jnp.bfloat16


def _params(sem, vmem=VMEM_LIMIT):
    return pltpu.CompilerParams(dimension_semantics=sem, vmem_limit_bytes=vmem)


def _dot(a, b):
    return jnp.dot(a, b, preferred_element_type=F32)


def _dot_nt(a, b):
    return lax.dot_general(a, b, (((1,), (1,)), ((), ())), preferred_element_type=F32)


def _silu(x):
    return x * jax.nn.sigmoid(x)


def _tile(n, pref):
    if n <= pref:
        return n
    t = (pref // LANES) * LANES
    while t >= LANES:
        if n % t == 0:
            return t
        t -= LANES
    raise ValueError("no lane-aligned tile for %d" % n)


def _layout():
    na_w = NA_HEADS * NA_HEAD_DIM
    ret_qk_w = RET_HEADS * RET_QK_DIM
    ret_v_w = RET_HEADS * RET_V_DIM
    diff_qk_w = DIFF_HEADS * 2 * DIFF_QK_DIM
    diff_v_w = DIFF_HEADS * DIFF_V_DIM
    names = ('qa', 'ka', 'va', 'qb', 'kb', 'vb', 'gb', 'qd', 'kd', 'vd', 'gates')
    widths = (na_w, na_w, na_w, ret_qk_w, ret_qk_w, ret_v_w, ret_v_w, diff_qk_w, diff_qk_w, diff_v_w,
              N_BRANCHES * D_MODEL)
    off = {}
    acc = 0
    for n, w in zip(names, widths):
        off[n] = acc
        acc += w
    return off, acc


def _normmod(x, g, shift, scale):
    ms = jnp.mean(x * x, axis=-1, keepdims=True)
    y = x * lax.rsqrt(ms + NORM_EPS) * g
    return y * (1.0 + scale) + shift


def _ada_kernel(c_ref, w_ref, b_ref, o_ref):
    a = _silu(c_ref[...]).astype(BF16)
    o_ref[...] = _dot(a, w_ref[...].astype(BF16)) + b_ref[...]


def _adaln(cond8, w_ada, b_ada):
    d, n = w_ada.shape
    tn = _tile(n, 1024)
    return pl.pallas_call(
        _ada_kernel,
        grid=(n // tn,),
        in_specs=[pl.BlockSpec((8, d), lambda j: (0, 0)),
                  pl.BlockSpec((d, tn), lambda j: (0, j)),
                  pl.BlockSpec((1, tn), lambda j: (0, j))],
        out_specs=pl.BlockSpec((8, tn), lambda j: (0, j)),
        out_shape=jax.ShapeDtypeStruct((8, n), F32),
        compiler_params=_params(("parallel",)),
    )(cond8, w_ada, b_ada.reshape(1, n))


def _normmod_mm_kernel(h_ref, mod_ref, g_ref, w_ref, o_ref, u_ref, *, row):
    @pl.when(pl.program_id(1) == 0)
    def _():
        mod = mod_ref[0]
        u = _normmod(h_ref[...], g_ref[...], mod[row:row + 1], mod[row + 1:row + 2])
        u_ref[...] = u.astype(BF16)

    o_ref[...] = _dot(u_ref[...], w_ref[...]).astype(o_ref.dtype)


def _normmod_swiglu_kernel(h_ref, mod_ref, g_ref, w1_ref, w3_ref, o_ref, u_ref, *, row):
    @pl.when(pl.program_id(1) == 0)
    def _():
        mod = mod_ref[0]
        u = _normmod(h_ref[...], g_ref[...], mod[row:row + 1], mod[row + 1:row + 2])
        u_ref[...] = u.astype(BF16)

    u = u_ref[...]
    o_ref[...] = (_silu(_dot(u, w1_ref[...])) * _dot(u, w3_ref[...])).astype(o_ref.dtype)


def _mod_index(tm, seq, batch):
    return lambda i, j: (jnp.minimum((i * tm) // seq, batch), 0, 0)


def _normmod_mm(h, mod, g, w, *, row, n_rows, tm, seq, batch, tn_pref):
    d, n = w.shape
    tn = _tile(n, tn_pref)
    return pl.pallas_call(
        functools.partial(_normmod_mm_kernel, row=row),
        grid=(n_rows // tm, n // tn),
        in_specs=[pl.BlockSpec((tm, d), lambda i, j: (i, 0)),
                  pl.BlockSpec((1, 8, d), _mod_index(tm, seq, batch)),
                  pl.BlockSpec((1, d), lambda i, j: (0, 0)),
                  pl.BlockSpec((d, tn), lambda i, j: (0, j))],
        out_specs=pl.BlockSpec((tm, tn), lambda i, j: (i, j)),
        out_shape=jax.ShapeDtypeStruct((h.shape[0], n), BF16),
        scratch_shapes=[pltpu.VMEM((tm, d), BF16)],
        compiler_params=_params(("parallel", "arbitrary")),
    )(h, mod, g.reshape(1, d), w)


def _normmod_swiglu(h, mod, g, w1, w3, *, row, n_rows, tm, seq, batch):
    d, n = w1.shape
    tn = _tile(n, 512)
    return pl.pallas_call(
        functools.partial(_normmod_swiglu_kernel, row=row),
        grid=(n_rows // tm, n // tn),
        in_specs=[pl.BlockSpec((tm, d), lambda i, j: (i, 0)),
                  pl.BlockSpec((1, 8, d), _mod_index(tm, seq, batch)),
                  pl.BlockSpec((1, d), lambda i, j: (0, 0)),
                  pl.BlockSpec((d, tn), lambda i, j: (0, j)),
                  pl.BlockSpec((d, tn), lambda i, j: (0, j))],
        out_specs=pl.BlockSpec((tm, tn), lambda i, j: (i, j)),
        out_shape=jax.ShapeDtypeStruct((h.shape[0], n), BF16),
        scratch_shapes=[pltpu.VMEM((tm, d), BF16)],
        compiler_params=_params(("parallel", "arbitrary")),
    )(h, mod, g.reshape(1, d), w1, w3)


def _mm_res_kernel(a_ref, w_ref, h_ref, mod_ref, o_ref, acc_ref, *, row, nk):
    k = pl.program_id(2)

    @pl.when(k == 0)
    def _():
        acc_ref[...] = jnp.zeros_like(acc_ref)

    acc_ref[...] += _dot(a_ref[...], w_ref[...])

    @pl.when(k == nk - 1)
    def _():
        gate = mod_ref[0][row:row + 1]
        o_ref[...] = h_ref[...] + gate * acc_ref[...]


def _mm_res(a, w, h, mod, *, row, n_rows, tm, seq, batch):
    kdim, n = w.shape
    tn = _tile(n, 1024)
    tk = _tile(kdim, 2048)
    nk = kdim // tk
    mod_idx = _mod_index(tm, seq, batch)
    return pl.pallas_call(
        functools.partial(_mm_res_kernel, row=row, nk=nk),
        grid=(n_rows // tm, n // tn, nk),
        in_specs=[pl.BlockSpec((tm, tk), lambda i, j, k: (i, k)),
                  pl.BlockSpec((tk, tn), lambda i, j, k: (k, j)),
                  pl.BlockSpec((tm, tn), lambda i, j, k: (i, j)),
                  pl.BlockSpec((1, 8, tn), lambda i, j, k: (mod_idx(i, j)[0], 0, j))],
        out_specs=pl.BlockSpec((tm, tn), lambda i, j, k: (i, j)),
        out_shape=jax.ShapeDtypeStruct(h.shape, F32),
        scratch_shapes=[pltpu.VMEM((tm, tn), F32)],
        compiler_params=_params(("parallel", "parallel", "arbitrary")),
    )(a, w, h, mod)


def _rope_tables(seq):
    t = jnp.arange(seq)
    row = (t // GRID_W).astype(F32)
    col = (t % GRID_W).astype(F32)
    axis_dim = DIFF_QK_DIM // 2
    inv_freq = ROPE_BASE ** (-jnp.arange(0, axis_dim, 2, dtype=F32) / axis_dim)
    ar = row[:, None] * inv_freq
    ac = col[:, None] * inv_freq
    ang = jnp.concatenate([ar, ar, ac, ac], axis=-1)
    reps = LANES // DIFF_QK_DIM
    cos = jnp.tile(jnp.cos(ang), (1, reps))
    sin = jnp.tile(jnp.sin(ang), (1, reps))
    half = DIFF_QK_DIM // 4
    first = (jnp.arange(LANES) % (2 * half)) < half
    sin_dn = jnp.where(first, 0.0, sin)
    sin_up = jnp.where(first, -sin, 0.0)
    return cos, sin_dn, sin_up


def _rope_kernel(q_ref, k_ref, cos_ref, sdn_ref, sup_ref, qo_ref, ko_ref):
    cos = cos_ref[...]
    sdn = sdn_ref[...]
    sup = sup_ref[...]
    half = DIFF_QK_DIM // 4
    for src, dst in ((q_ref, qo_ref), (k_ref, ko_ref)):
        for c in range(src.shape[1] // LANES):
            x = src[:, c * LANES:(c + 1) * LANES].astype(F32)
            y = x * cos + pltpu.roll(x, half, 1) * sdn + pltpu.roll(x, LANES - half, 1) * sup
            dst[:, c * LANES:(c + 1) * LANES] = y.astype(dst.dtype)


def _rope(p, tabs, *, off, n_lat, seq, tm):
    w = DIFF_HEADS * 2 * DIFF_QK_DIM
    assert off['qd'] % w == 0 and off['kd'] % w == 0
    per = seq // tm
    tab_spec = pl.BlockSpec((tm, LANES), lambda i: (i % per, 0))
    return pl.pallas_call(
        _rope_kernel,
        grid=(n_lat // tm,),
        in_specs=[pl.BlockSpec((tm, w), lambda i: (i, off['qd'] // w)),
                  pl.BlockSpec((tm, w), lambda i: (i, off['kd'] // w)),
                  tab_spec, tab_spec, tab_spec],
        out_specs=[pl.BlockSpec((tm, w), lambda i: (i, 0)), pl.BlockSpec((tm, w), lambda i: (i, 0))],
        out_shape=[jax.ShapeDtypeStruct((n_lat, w), BF16), jax.ShapeDtypeStruct((n_lat, w), BF16)],
        compiler_params=_params(("parallel",)),
    )(p, p, *tabs)


def _na_bias_table(rpb, rows):
    kh = NA_WIN_H
    cols = jnp.arange(GRID_W)
    c0 = jnp.clip(cols - NA_WIN_W // 2, 0, GRID_W - NA_WIN_W)
    col_in = (cols[None, :] >= c0[:, None]) & (cols[None, :] < c0[:, None] + NA_WIN_W)
    dc_idx = jnp.clip(cols[None, :] - cols[:, None] + (NA_WIN_W - 1), 0, 2 * NA_WIN_W - 2)
    offs = jnp.arange(kh)
    dr_idx = jnp.arange(kh)[None, :] - offs[:, None] + (NA_WIN_H - 1)
    bias = rpb[:, dr_idx[:, None, :, None], dc_idx[None, :, None, :]].astype(F32)
    bias = jnp.where(col_in[None, None, :, None, :], bias, NEG_INF)
    return bias.reshape(rpb.shape[0], kh, GRID_W, kh * GRID_W)


def _na_kernel(q_ref, k_ref, v_ref, kc_ref, vc_ref, bias_ref, o_ref, *, rb, rows, scale):
    blk = pl.program_id(2)
    kc = kc_ref[...]
    vc = vc_ref[...]
    win = NA_WIN_H * GRID_W
    for j in range(rb):
        r = blk * rb + j
        r0 = jnp.clip(r - NA_WIN_H // 2, 0, rows - NA_WIN_H)
        start = pl.multiple_of(r0 * GRID_W, GRID_W)
        kw = k_ref[pl.ds(start, win), :]
        vw = v_ref[pl.ds(start, win), :]
        q = q_ref[j * GRID_W:(j + 1) * GRID_W, :]
        s_l = _dot_nt(q, kw) * scale + bias_ref[0, r - r0]
        s_c = _dot_nt(q, kc) * scale
        m = jnp.maximum(jnp.max(s_l, axis=-1, keepdims=True), jnp.max(s_c, axis=-1, keepdims=True))
        p_l = jnp.exp(s_l - m)
        p_c = jnp.exp(s_c - m)
        den = jnp.sum(p_l, axis=-1, keepdims=True) + jnp.sum(p_c, axis=-1, keepdims=True)
        o = _dot(p_l.astype(BF16), vw) + _dot(p_c.astype(BF16), vc)
        o_ref[j * GRID_W:(j + 1) * GRID_W, :] = (o / den).astype(o_ref.dtype)


def _na_attention(p, bias_tab, *, off, batch, seq, ctx, n_tok):
    rows = seq // GRID_W
    assert rows >= NA_WIN_H
    rb = 4 if rows % 4 == 0 else 1
    dh = NA_HEAD_DIM
    nblk = rows // rb
    qc, kc_, vc_ = off['qa'] // dh, off['ka'] // dh, off['va'] // dh
    ctx_blk0 = batch * seq // ctx
    win = NA_WIN_H * GRID_W
    return pl.pallas_call(
        functools.partial(_na_kernel, rb=rb, rows=rows, scale=dh ** -0.5),
        grid=(batch, NA_HEADS, nblk),
        in_specs=[pl.BlockSpec((rb * GRID_W, dh), lambda b, h, r: (b * nblk + r, qc + h)),
                  pl.BlockSpec((seq, dh), lambda b, h, r: (b, kc_ + h)),
                  pl.BlockSpec((seq, dh), lambda b, h, r: (b, vc_ + h)),
                  pl.BlockSpec((ctx, dh), lambda b, h, r: (ctx_blk0 + b, kc_ + h)),
                  pl.BlockSpec((ctx, dh), lambda b, h, r: (ctx_blk0 + b, vc_ + h)),
                  pl.BlockSpec((1, NA_WIN_H, GRID_W, win), lambda b, h, r: (h, 0, 0, 0))],
        out_specs=pl.BlockSpec((rb * GRID_W, dh), lambda b, h, r: (b * nblk + r, h)),
        out_shape=jax.ShapeDtypeStruct((n_tok, NA_HEADS * dh), BF16),
        compiler_params=_params(("parallel", "parallel", "arbitrary")),
    )(p, p, p, p, p, bias_tab)


def _ctx_na_kernel(q_ref, k_ref, v_ref, prev_ref, o_ref, *, scale):
    del prev_ref
    s = _dot_nt(q_ref[...], k_ref[...]) * scale
    m = jnp.max(s, axis=-1, keepdims=True)
    e = jnp.exp(s - m)
    den = jnp.sum(e, axis=-1, keepdims=True)
    o_ref[...] = (_dot(e.astype(BF16), v_ref[...]) / den).astype(o_ref.dtype)


def _ctx_na_attention(p, oa, *, off, batch, seq, ctx):
    dh = NA_HEAD_DIM
    qc, kc_, vc_ = off['qa'] // dh, off['ka'] // dh, off['va'] // dh
    blk0 = batch * seq // ctx
    return pl.pallas_call(
        functools.partial(_ctx_na_kernel, scale=dh ** -0.5),
        grid=(batch, NA_HEADS),
        in_specs=[pl.BlockSpec((ctx, dh), lambda b, h: (blk0 + b, qc + h)),
                  pl.BlockSpec((ctx, dh), lambda b, h: (blk0 + b, kc_ + h)),
                  pl.BlockSpec((ctx, dh), lambda b, h: (blk0 + b, vc_ + h)),
                  pl.BlockSpec(memory_space=pl.ANY)],
        out_specs=pl.BlockSpec((ctx, dh), lambda b, h: (blk0 + b, h)),
        out_shape=jax.ShapeDtypeStruct(oa.shape, oa.dtype),
        input_output_aliases={3: 0},
        compiler_params=_params(("parallel", "parallel")),
    )(p, p, p, oa)


def _split_maps(q):
    lane = lax.broadcasted_iota(jnp.int32, q.shape, 1)
    zero = jnp.zeros_like(q)
    return jnp.concatenate([jnp.where(lane < DIFF_QK_DIM, q, zero), jnp.where(lane >= DIFF_QK_DIM, q, zero)],
                           axis=0)


def _diff_finish(o1, o2, par_ref):
    lam = par_ref[1:2, :]
    o = o1 - lam * o2
    y = o * lax.rsqrt(jnp.mean(o * o, axis=-1, keepdims=True) + SUBLN_EPS)
    return y * par_ref[0:1, :]


def _diff_kernel(q_ref, k_ref, v_ref, kc_ref, vc_ref, par_ref, o_ref, *, tk, scale):
    tq = q_ref.shape[0]
    qq = _split_maps(q_ref[...])

    def step(kb, vb, carry):
        m, l, acc = carry
        s = _dot_nt(qq, kb) * scale
        m_new = jnp.maximum(m, jnp.max(s, axis=-1, keepdims=True))
        alpha = jnp.exp(m - m_new)
        e = jnp.exp(s - m_new)
        l = alpha * l + jnp.sum(e, axis=-1, keepdims=True)
        acc = alpha * acc + _dot(e.astype(BF16), vb)
        return m_new, l, acc

    def body(i, carry):
        start = pl.multiple_of(i * tk, tk)
        return step(k_ref[pl.ds(start, tk), :], v_ref[pl.ds(start, tk), :], carry)

    init = (jnp.full((2 * tq, 1), -jnp.inf, F32), jnp.zeros((2 * tq, 1), F32),
            jnp.zeros((2 * tq, v_ref.shape[1]), F32))
    carry = lax.fori_loop(0, k_ref.shape[0] // tk, body, init)
    m, l, acc = step(kc_ref[...], vc_ref[...], carry)
    o = acc / l
    o_ref[...] = _diff_finish(o[:tq], o[tq:], par_ref).astype(o_ref.dtype)


def _diff_attention(qr, kr, p, par, *, off, batch, seq, ctx, n_tok):
    dq2 = 2 * DIFF_QK_DIM
    dv = DIFF_V_DIM
    assert dq2 == LANES and dv == LANES
    tq = _tile(seq, 256)
    tk = _tile(seq, 512)
    nq = seq // tq
    kdc, vdc = off['kd'] // dq2, off['vd'] // dv
    ctx_blk0 = batch * seq // ctx
    return pl.pallas_call(
        functools.partial(_diff_kernel, tk=tk, scale=DIFF_QK_DIM ** -0.5),
        grid=(batch, DIFF_HEADS, nq),
        in_specs=[pl.BlockSpec((tq, dq2), lambda b, h, i: (b * nq + i, h)),
                  pl.BlockSpec((seq, dq2), lambda b, h, i: (b, h)),
                  pl.BlockSpec((seq, dv), lambda b, h, i: (b, vdc + h)),
                  pl.BlockSpec((ctx, dq2), lambda b, h, i: (ctx_blk0 + b, kdc + h)),
                  pl.BlockSpec((ctx, dv), lambda b, h, i: (ctx_blk0 + b, vdc + h)),
                  pl.BlockSpec((8, dv), lambda b, h, i: (0, 0))],
        out_specs=pl.BlockSpec((tq, dv), lambda b, h, i: (b * nq + i, h)),
        out_shape=jax.ShapeDtypeStruct((n_tok, DIFF_HEADS * dv), BF16),
        compiler_params=_params(("parallel", "parallel", "arbitrary")),
    )(qr, kr, p, p, p, par)


def _ctx_diff_kernel(q_ref, k_ref, v_ref, par_ref, prev_ref, o_ref, *, scale):
    del prev_ref
    n = q_ref.shape[0]
    s = _dot_nt(_split_maps(q_ref[...]), k_ref[...]) * scale
    m = jnp.max(s, axis=-1, keepdims=True)
    e = jnp.exp(s - m)
    den = jnp.sum(e, axis=-1, keepdims=True)
    o = _dot(e.astype(BF16), v_ref[...]) / den
    o_ref[...] = _diff_finish(o[:n], o[n:], par_ref).astype(o_ref.dtype)


def _ctx_diff_attention(p, od, par, *, off, batch, seq, ctx):
    dq2 = 2 * DIFF_QK_DIM
    dv = DIFF_V_DIM
    qdc, kdc, vdc = off['qd'] // dq2, off['kd'] // dq2, off['vd'] // dv
    blk0 = batch * seq // ctx
    return pl.pallas_call(
        functools.partial(_ctx_diff_kernel, scale=DIFF_QK_DIM ** -0.5),
        grid=(batch, DIFF_HEADS),
        in_specs=[pl.BlockSpec((ctx, dq2), lambda b, h: (blk0 + b, qdc + h)),
                  pl.BlockSpec((ctx, dq2), lambda b, h: (blk0 + b, kdc + h)),
                  pl.BlockSpec((ctx, dv), lambda b, h: (blk0 + b, vdc + h)),
                  pl.BlockSpec((8, dv), lambda b, h: (0, 0)),
                  pl.BlockSpec(memory_space=pl.ANY)],
        out_specs=pl.BlockSpec((ctx, dv), lambda b, h: (blk0 + b, h)),
        out_shape=jax.ShapeDtypeStruct(od.shape, od.dtype),
        input_output_aliases={4: 0},
        compiler_params=_params(("parallel", "parallel")),
    )(p, p, p, par, od)


def _ret_tables(log_gamma, reverse):
    cs = RET_CHUNK
    k_scale = RET_QK_DIM ** -0.5
    pos = jnp.arange(cs, dtype=F32)
    lg = log_gamma[:, None]
    rel = pos[:, None] - pos[None, :]
    if reverse:
        rel = -rel
        q_decay = jnp.exp(lg * (cs - pos))
        k_decay = jnp.exp(lg * pos)
    else:
        q_decay = jnp.exp(lg * (pos + 1.0))
        k_decay = jnp.exp(lg * (cs - 1.0 - pos))
    intra = jnp.where(rel >= 0, jnp.exp(lg[:, :, None] * jnp.maximum(rel, 0.0)), 0.0) * k_scale
    chunk_decay = jnp.broadcast_to(jnp.exp(lg * cs)[:, :, None], (RET_HEADS, 1, RET_V_DIM))
    return intra, q_decay[:, :, None], (k_decay * k_scale)[:, :, None], chunk_decay


def _ret_kernel(*refs, nh, final):
    q_refs, k_refs, v_refs = refs[0:nh], refs[nh:2 * nh], refs[2 * nh:3 * nh]
    pos = 3 * nh
    if final:
        g_refs = refs[pos:pos + nh]
        of_ref, gain_ref = refs[pos + nh:pos + nh + 2]
        pos += nh + 2
    intra_ref, qdec_ref, kdec_ref, cd_ref, o_ref, s_ref = refs[pos:pos + 6]
    dv = RET_V_DIM

    @pl.when(pl.program_id(1) == 0)
    def _():
        s_ref[...] = jnp.zeros_like(s_ref)

    for h in range(nh):
        q = q_refs[h][...]
        k = k_refs[h][...]
        v = v_refs[h][...]
        state = s_ref[h]
        att = _dot_nt(q, k) * intra_ref[h]
        qd = (q.astype(F32) * qdec_ref[h]).astype(BF16)
        o = _dot(att.astype(BF16), v) + _dot(qd, state.astype(BF16))
        kd_t = (k.astype(F32) * kdec_ref[h]).T.astype(BF16)
        s_ref[h] = state * cd_ref[h] + _dot(kd_t, v)
        cols = slice(h * dv, (h + 1) * dv)
        if final:
            o = o + of_ref[:, cols]
            mu = jnp.mean(o, axis=-1, keepdims=True)
            var = jnp.mean(jnp.square(o - mu), axis=-1, keepdims=True)
            y = (o - mu) * lax.rsqrt(var + NORM_EPS) * gain_ref[:, cols]
            o_ref[:, cols] = (_silu(g_refs[h][...].astype(F32)) * y).astype(o_ref.dtype)
        else:
            o_ref[:, cols] = o


def _retention(p, tabs_f, tabs_b, gain, *, off, batch, seq, ctx, n_tok):
    cs, nh, dk, dv = RET_CHUNK, RET_HEADS, RET_QK_DIM, RET_V_DIM
    lc, sc = ctx // cs, seq // cs
    ctx_blk0 = batch * seq // cs
    vw = nh * dv
    assert off['qb'] % dk == 0 and off['kb'] % dk == 0 and off['vb'] % dv == 0 and off['gb'] % dv == 0

    def chunk_fwd(b, t):
        return jnp.where(t < lc, ctx_blk0 + b * lc + t, b * sc + (t - lc))

    def chunk_bwd(b, t):
        return jnp.where(t < lc, ctx_blk0 + b * lc + (lc - 1 - t), b * sc + (sc - 1 - (t - lc)))

    def run(chunk, tabs, final, of):
        def head_spec(width, col0, h):
            return pl.BlockSpec((cs, width), lambda b, t: (chunk(b, t), col0 + h))

        in_specs = ([head_spec(dk, off['qb'] // dk, h) for h in range(nh)]
                    + [head_spec(dk, off['kb'] // dk, h) for h in range(nh)]
                    + [head_spec(dv, off['vb'] // dv, h) for h in range(nh)])
        args = [p] * (3 * nh)
        if final:
            in_specs += [head_spec(dv, off['gb'] // dv, h) for h in range(nh)]
            in_specs += [pl.BlockSpec((cs, vw), lambda b, t: (chunk(b, t), 0)),
                         pl.BlockSpec((1, vw), lambda b, t: (0, 0))]
            args += [p] * nh + [of, gain.reshape(1, vw).astype(F32)]
        in_specs += [pl.BlockSpec((nh, cs, cs), lambda b, t: (0, 0, 0)),
                     pl.BlockSpec((nh, cs, 1), lambda b, t: (0, 0, 0)),
                     pl.BlockSpec((nh, cs, 1), lambda b, t: (0, 0, 0)),
                     pl.BlockSpec((nh, 1, dv), lambda b, t: (0, 0, 0))]
        args += list(tabs)
        return pl.pallas_call(
            functools.partial(_ret_kernel, nh=nh, final=final),
            grid=(batch, lc + sc),
            in_specs=in_specs,
            out_specs=pl.BlockSpec((cs, vw), lambda b, t: (chunk(b, t), 0)),
            out_shape=jax.ShapeDtypeStruct((n_tok, vw), BF16 if final else F32),
            scratch_shapes=[pltpu.VMEM((nh, dk, dv), F32)],
            compiler_params=_params(("parallel", "arbitrary")),
        )(*args)

    o_f = run(chunk_fwd, tabs_f, False, None)
    return run(chunk_bwd, tabs_b, True, o_f)


def _merge_kernel(oa_ref, ob_ref, od_ref, ga_ref, gb_ref, gd_ref, wa_ref, wb_ref, wd_ref, o_ref):
    y = jax.nn.sigmoid(ga_ref[...].astype(F32)) * _dot(oa_ref[...], wa_ref[...])
    y += jax.nn.sigmoid(gb_ref[...].astype(F32)) * _dot(ob_ref[...], wb_ref[...])
    y += jax.nn.sigmoid(gd_ref[...].astype(F32)) * _dot(od_ref[...], wd_ref[...])
    o_ref[...] = y.astype(o_ref.dtype)


def _merge(oa, ob, od, p, wa, wb, wd, *, off, n_rows, tm):
    d = D_MODEL
    tn = _tile(d, 512)
    assert off['gates'] % tn == 0
    g0 = off['gates'] // tn
    nj = d // tn

    def full(a):
        return pl.BlockSpec((tm, a.shape[1]), lambda i, j: (i, 0))

    def wspec(w):
        return pl.BlockSpec((w.shape[0], tn), lambda i, j: (0, j))

    def gspec(br):
        return pl.BlockSpec((tm, tn), lambda i, j: (i, g0 + br * nj + j))

    return pl.pallas_call(
        _merge_kernel,
        grid=(n_rows // tm, nj),
        in_specs=[full(oa), full(ob), full(od), gspec(0), gspec(1), gspec(2), wspec(wa), wspec(wb), wspec(wd)],
        out_specs=pl.BlockSpec((tm, tn), lambda i, j: (i, j)),
        out_shape=jax.ShapeDtypeStruct((oa.shape[0], d), BF16),
        compiler_params=_params(("parallel", "arbitrary")),
    )(oa, ob, od, p, p, p, wa, wb, wd)


def _router_kernel(h_ref, mod_ref, g_ref, rw_ref, rb_ref, u_ref, idx_ref, wt_ref, *, row):
    mod = mod_ref[0]
    u = _normmod(h_ref[...], g_ref[...], mod[row:row + 1], mod[row + 1:row + 2])
    u_hi = u.astype(BF16)
    u_ref[...] = u_hi
    u_lo = (u - u_hi.astype(F32)).astype(BF16)
    w = rw_ref[...]
    w_hi = w.astype(BF16)
    w_lo = (w - w_hi.astype(F32)).astype(BF16)
    logits = _dot(u_hi, w_hi) + _dot(u_hi, w_lo) + _dot(u_lo, w_hi) + rb_ref[...]
    lane = lax.broadcasted_iota(jnp.int32, logits.shape, 1)
    lg = jnp.where(lane < N_EXPERTS, logits, -jnp.inf)
    v1 = jnp.max(lg, axis=-1, keepdims=True)
    lane_f = lane.astype(F32)
    i1 = jnp.min(jnp.where(lg == v1, lane_f, float(LANES)), axis=-1, keepdims=True).astype(jnp.int32)
    lg2 = jnp.where(lane == i1, -jnp.inf, lg)
    v2 = jnp.max(lg2, axis=-1, keepdims=True)
    i2 = jnp.min(jnp.where(lg2 == v2, lane_f, float(LANES)), axis=-1, keepdims=True).astype(jnp.int32)
    e = jnp.exp(v2 - v1)
    w1 = 1.0 / (1.0 + e)
    w2 = e / (1.0 + e)
    idx_ref[...] = jnp.where(lane == 0, i1, jnp.where(lane == 1, i2, 0))
    wt_ref[...] = jnp.where(lane == 0, w1, jnp.where(lane == 1, w2, 0.0))


def _router(h, mod, g, router_w, router_b, *, row, n_rows, tm, seq, batch):
    d = D_MODEL
    rw = jnp.zeros((d, LANES), F32).at[:, :N_EXPERTS].set(router_w.astype(F32))
    rb = jnp.zeros((1, LANES), F32).at[0, :N_EXPERTS].set(router_b.astype(F32))
    mod_idx = _mod_index(tm, seq, batch)
    return pl.pallas_call(
        functools.partial(_router_kernel, row=row),
        grid=(n_rows // tm,),
        in_specs=[pl.BlockSpec((tm, d), lambda i: (i, 0)),
                  pl.BlockSpec((1, 8, d), lambda i: mod_idx(i, 0)),
                  pl.BlockSpec((1, d), lambda i: (0, 0)),
                  pl.BlockSpec((d, LANES), lambda i: (0, 0)),
                  pl.BlockSpec((1, LANES), lambda i: (0, 0))],
        out_specs=[pl.BlockSpec((tm, d), lambda i: (i, 0)),
                   pl.BlockSpec((tm, LANES), lambda i: (i, 0)),
                   pl.BlockSpec((tm, LANES), lambda i: (i, 0))],
        out_shape=[jax.ShapeDtypeStruct((n_rows, d), BF16),
                   jax.ShapeDtypeStruct((n_rows, LANES), jnp.int32),
                   jax.ShapeDtypeStruct((n_rows, LANES), F32)],
        compiler_params=_params(("parallel",)),
    )(h, mod, g.reshape(1, d), rw, rb)


def _route_plan(ridx, rwt, tme):
    n_tok = ridx.shape[0]
    n2 = n_tok * TOP_K
    n_pad = n2 + N_EXPERTS * tme
    n_tiles = n_pad // tme
    e_flat = ridx.reshape(-1)
    order = jnp.argsort(e_flat, stable=True).astype(jnp.int32)
    counts = jnp.sum(e_flat[:, None] == jnp.arange(N_EXPERTS)[None, :], axis=0).astype(jnp.int32)
    padded = ((counts + tme - 1) // tme) * tme
    pend = jnp.cumsum(padded)
    pstart = pend - padded
    ustart = jnp.cumsum(counts) - counts
    tile_start = jnp.arange(n_tiles, dtype=jnp.int32) * tme
    tile_expert = jnp.minimum(jnp.searchsorted(pend, tile_start, side='right'), N_EXPERTS - 1).astype(jnp.int32)
    tile_valid = (tile_start < pend[-1]).astype(jnp.int32)
    slot = jnp.arange(n_pad, dtype=jnp.int32)
    slot_e = jnp.repeat(tile_expert, tme)
    within = slot - pstart[slot_e]
    valid = (within < counts[slot_e]) & (jnp.repeat(tile_valid, tme) > 0)
    src = jnp.clip(ustart[slot_e] + within, 0, n2 - 1)
    assign = order[src]
    perm_tok = jnp.where(valid, assign // TOP_K, 0)
    w_sorted = jnp.where(valid, rwt.reshape(-1)[assign], 0.0)
    e_sorted = e_flat[order]
    pos = jnp.arange(n2, dtype=jnp.int32) - ustart[e_sorted] + pstart[e_sorted]
    slot_of = pos[jnp.argsort(order)].reshape(n_tok, TOP_K)
    return perm_tok, w_sorted, slot_of, tile_expert, tile_valid


def _moe_up_kernel(te_ref, tv_ref, a_ref, w1_ref, w3_ref, o_ref):
    i = pl.program_id(1)

    @pl.when(tv_ref[i] > 0)
    def _():
        a = a_ref[...]
        o_ref[...] = (_silu(_dot(a, w1_ref[0])) * _dot(a, w3_ref[0])).astype(o_ref.dtype)

    @pl.when(tv_ref[i] == 0)
    def _():
        o_ref[...] = jnp.zeros_like(o_ref)


def _moe_up(a, w1, w3, tile_expert, tile_valid, *, tme):
    n_pad, d = a.shape
    f = w1.shape[2]
    tn = _tile(f, 1024)
    return pl.pallas_call(
        _moe_up_kernel,
        grid_spec=pltpu.PrefetchScalarGridSpec(
            num_scalar_prefetch=2,
            grid=(f // tn, n_pad // tme),
            in_specs=[pl.BlockSpec((tme, d), lambda j, i, te, tv: (i, 0)),
                      pl.BlockSpec((1, d, tn), lambda j, i, te, tv: (te[i], 0, j)),
                      pl.BlockSpec((1, d, tn), lambda j, i, te, tv: (te[i], 0, j))],
            out_specs=pl.BlockSpec((tme, tn), lambda j, i, te, tv: (i, j))),
        out_shape=jax.ShapeDtypeStruct((n_pad, f), BF16),
        compiler_params=_params(("parallel", "arbitrary")),
    )(tile_expert, tile_valid, a, w1, w3)


def _moe_down_kernel(te_ref, tv_ref, a_ref, w_ref, ws_ref, o_ref, acc_ref, *, nk):
    i = pl.program_id(0)
    k = pl.program_id(1)

    @pl.when(k == 0)
    def _():
        acc_ref[...] = jnp.zeros_like(acc_ref)

    @pl.when(tv_ref[i] > 0)
    def _():
        acc_ref[...] += _dot(a_ref[...], w_ref[0])

    @pl.when(k == nk - 1)
    def _():
        o_ref[...] = (ws_ref[...] * acc_ref[...]).astype(o_ref.dtype)


def _moe_down(a, w2, w_sorted, tile_expert, tile_valid, *, tme):
    n_pad, f = a.shape
    d = w2.shape[2]
    tk = _tile(f, 1024)
    nk = f // tk
    return pl.pallas_call(
        functools.partial(_moe_down_kernel, nk=nk),
        grid_spec=pltpu.PrefetchScalarGridSpec(
            num_scalar_prefetch=2,
            grid=(n_pad // tme, nk),
            in_specs=[pl.BlockSpec((tme, tk), lambda i, k, te, tv: (i, k)),
                      pl.BlockSpec((1, tk, d), lambda i, k, te, tv: (te[i], k, 0)),
                      pl.BlockSpec((tme, 1), lambda i, k, te, tv: (i, 0))],
            out_specs=pl.BlockSpec((tme, d), lambda i, k, te, tv: (i, 0)),
            scratch_shapes=[pltpu.VMEM((tme, d), F32)]),
        out_shape=jax.ShapeDtypeStruct((n_pad, d), BF16),
        compiler_params=_params(("parallel", "arbitrary")),
    )(tile_expert, tile_valid, a, w2, w_sorted.reshape(n_pad, 1))


def _combine_norm_kernel(h_ref, y1_ref, y2_ref, mod_ref, g_ref, o_ref, *, row):
    gate = mod_ref[0][row:row + 1]
    x = h_ref[...] + gate * (y1_ref[...].astype(F32) + y2_ref[...].astype(F32))
    ms = jnp.mean(x * x, axis=-1, keepdims=True)
    o_ref[...] = x * lax.rsqrt(ms + NORM_EPS) * g_ref[...]


def _combine_norm(h, y1, y2, mod, g, *, row, n_rows, tm, seq, batch):
    d = D_MODEL
    mod_idx = _mod_index(tm, seq, batch)
    rowspec = pl.BlockSpec((tm, d), lambda i: (i, 0))
    return pl.pallas_call(
        functools.partial(_combine_norm_kernel, row=row),
        grid=(n_rows // tm,),
        in_specs=[rowspec, rowspec, rowspec,
                  pl.BlockSpec((1, 8, d), lambda i: mod_idx(i, 0)),
                  pl.BlockSpec((1, d), lambda i: (0, 0))],
        out_specs=rowspec,
        out_shape=jax.ShapeDtypeStruct((n_rows, d), F32),
        compiler_params=_params(("parallel",)),
    )(h, y1, y2, mod, g.reshape(1, d))


def _token_mixers(h, mod, lp, layer_idx, rope_tabs, *, dims, need_ctx):
    batch, seq, ctx, n_tok, n_lat, tm = dims
    off, in_w = _layout()
    bf = lambda a: a.astype(BF16)
    p = _normmod_mm(h, mod, lp['norm1'], bf(lp['w_in']), row=0, n_rows=n_tok, tm=tm, seq=seq, batch=batch,
                    tn_pref=768)

    bias_tab = _na_bias_table(lp['na_rpb'], seq // GRID_W)
    oa = _na_attention(p, bias_tab, off=off, batch=batch, seq=seq, ctx=ctx, n_tok=n_tok)

    log_gamma = jax.nn.log_sigmoid(lp['ret_decay'].astype(F32))
    ob = _retention(p, _ret_tables(log_gamma[0], False), _ret_tables(log_gamma[1], True), lp['ret_gn'],
                    off=off, batch=batch, seq=seq, ctx=ctx, n_tok=n_tok)

    lam_init = 0.8 - 0.6 * math.exp(-0.3 * layer_idx)
    lq1, lk1, lq2, lk2 = lp['diff_lam'].astype(F32)
    lam = jnp.exp(jnp.sum(lq1 * lk1)) - jnp.exp(jnp.sum(lq2 * lk2)) + lam_init
    par = jnp.zeros((8, DIFF_V_DIM), F32)
    par = par.at[0].set(lp['diff_subln'].astype(F32) * (1.0 - lam_init)).at[1].set(lam)
    qr, kr = _rope(p, rope_tabs, off=off, n_lat=n_lat, seq=seq, tm=tm)
    od = _diff_attention(qr, kr, p, par, off=off, batch=batch, seq=seq, ctx=ctx, n_tok=n_tok)

    if need_ctx:
        oa = _ctx_na_attention(p, oa, off=off, batch=batch, seq=seq, ctx=ctx)
        od = _ctx_diff_attention(p, od, par, off=off, batch=batch, seq=seq, ctx=ctx)
    n_rows = n_tok if need_ctx else n_lat
    ymid = _merge(oa, ob, od, p, bf(lp['w_br_a']), bf(lp['w_br_b']), bf(lp['w_br_c']), off=off, n_rows=n_rows, tm=tm)
    return _mm_res(ymid, bf(lp['w_out']), h, mod, row=2, n_rows=n_rows, tm=tm, seq=seq, batch=batch)


def _forward(x, c, ctx_tok, c_ctx, layers, final_norm):
    batch, seq, d = x.shape
    ctx = ctx_tok.shape[1]
    n_lat = batch * seq
    n_tok = n_lat + batch * ctx
    tm = min(ROW_TILE, batch * ctx)
    assert d == D_MODEL and seq % tm == 0 and (batch * ctx) % tm == 0 and seq % GRID_W == 0
    assert seq % RET_CHUNK == 0 and ctx % RET_CHUNK == 0 and n_lat % ctx == 0 and batch + 1 <= 8
    dims = (batch, seq, ctx, n_tok, n_lat, tm)
    bf = lambda a: a.astype(BF16)

    h = jnp.concatenate([x.reshape(n_lat, d), ctx_tok.reshape(batch * ctx, d)], axis=0).astype(F32)
    cond8 = jnp.zeros((8, d), F32).at[:batch].set(c).at[batch].set(c_ctx)
    rope_tabs = _rope_tables(seq)
    n_layers = len(layers)
    out = None
    for li, lp in enumerate(layers):
        need_ctx = li < n_layers - 1
        m = _adaln(cond8, lp['w_ada'], lp['b_ada'])
        mod = jnp.zeros((batch + 1, 8, d), F32).at[:, :6].set(m[:batch + 1].reshape(batch + 1, 6, d))
        h = _token_mixers(h, mod, lp, li, rope_tabs, dims=dims, need_ctx=need_ctx)
        n_rows = n_tok if need_ctx else n_lat
        if 'ffn_w1' in lp:
            a = _normmod_swiglu(h, mod, lp['norm2'], bf(lp['ffn_w1']), bf(lp['ffn_w3']), row=3, n_rows=n_rows,
                                tm=tm, seq=seq, batch=batch)
            h = _mm_res(a, bf(lp['ffn_w2']), h, mod, row=5, n_rows=n_rows, tm=tm, seq=seq, batch=batch)
        else:
            assert not need_ctx and li == n_layers - 1
            tme = min(MOE_ROW_TILE, n_lat)
            u, ridx, rwt = _router(h, mod, lp['norm2'], lp['router_w'], lp['router_b'], row=3, n_rows=n_lat,
                                   tm=tm, seq=seq, batch=batch)
            perm_tok, w_sorted, slot_of, tile_expert, tile_valid = _route_plan(ridx[:, :TOP_K], rwt[:, :TOP_K], tme)
            a = _moe_up(jnp.take(u, perm_tok, axis=0), bf(lp['exp_w1']), bf(lp['exp_w3']), tile_expert, tile_valid,
                        tme=tme)
            y = _moe_down(a, bf(lp['exp_w2']), w_sorted, tile_expert, tile_valid, tme=tme)
            out = _combine_norm(h, jnp.take(y, slot_of[:, 0], axis=0), jnp.take(y, slot_of[:, 1], axis=0), mod,
                                final_norm, row=5, n_rows=n_lat, tm=tm, seq=seq, batch=batch)
    return out.reshape(batch, seq, d)


def kernel(x, c, ctx, c_ctx, l0_w_ada, l0_b_ada, l0_norm1, l0_w_in, l0_na_rpb, l0_ret_decay, l0_ret_gn, l0_diff_lam, l0_diff_subln, l0_w_br_a, l0_w_br_b, l0_w_br_c, l0_w_out, l0_norm2, l0_ffn_w1, l0_ffn_w3, l0_ffn_w2, l1_w_ada, l1_b_ada, l1_norm1, l1_w_in, l1_na_rpb, l1_ret_decay, l1_ret_gn, l1_diff_lam, l1_diff_subln, l1_w_br_a, l1_w_br_b, l1_w_br_c, l1_w_out, l1_norm2, l1_router_w, l1_router_b, l1_exp_w1, l1_exp_w3, l1_exp_w2, final_norm):
    layers = (
        dict(w_ada=l0_w_ada, b_ada=l0_b_ada, norm1=l0_norm1, w_in=l0_w_in, na_rpb=l0_na_rpb,
             ret_decay=l0_ret_decay, ret_gn=l0_ret_gn, diff_lam=l0_diff_lam, diff_subln=l0_diff_subln,
             w_br_a=l0_w_br_a, w_br_b=l0_w_br_b, w_br_c=l0_w_br_c, w_out=l0_w_out, norm2=l0_norm2,
             ffn_w1=l0_ffn_w1, ffn_w3=l0_ffn_w3, ffn_w2=l0_ffn_w2),
        dict(w_ada=l1_w_ada, b_ada=l1_b_ada, norm1=l1_norm1, w_in=l1_w_in, na_rpb=l1_na_rpb,
             ret_decay=l1_ret_decay, ret_gn=l1_ret_gn, diff_lam=l1_diff_lam, diff_subln=l1_diff_subln,
             w_br_a=l1_w_br_a, w_br_b=l1_w_br_b, w_br_c=l1_w_br_c, w_out=l1_w_out, norm2=l1_norm2,
             router_w=l1_router_w, router_b=l1_router_b, exp_w1=l1_exp_w1, exp_w3=l1_exp_w3,
             exp_w2=l1_exp_w2),
    )
    return _forward(x, c, ctx, c_ctx, layers, final_norm)
```

```python
import functools
import math

import jax
import jax.numpy as jnp
from jax import lax
from jax.experimental import pallas as pl
from jax.experimental.pallas import tpu as pltpu

D_MODEL = 2048
GRID_W = 64
NA_HEADS = 6
NA_HEAD_DIM = 128
NA_WIN_H = 8
NA_WIN_W = 16
RET_HEADS = 4
RET_QK_DIM = 128
RET_V_DIM = 256
RET_CHUNK = 128
DIFF_HEADS = 6
DIFF_QK_DIM = 64
DIFF_V_DIM = 128
ROPE_BASE = 10000.0
N_EXPERTS = 8
TOP_K = 2
N_BRANCHES = 3
NORM_EPS = 1e-6
SUBLN_EPS = 1e-5
NEG_INF = -1e30

LANES = 128
ROW_TILE = 512
MOE_ROW_TILE = 512
VMEM_LIMIT = 56 << 20

F32 = jnp.float32
BF16 = jnp.bfloat16


def _params(sem, vmem=VMEM_LIMIT):
    return pltpu.CompilerParams(dimension_semantics=sem, vmem_limit_bytes=vmem)


def _dot(a, b):
    return jnp.dot(a, b, preferred_element_type=F32)


def _dot_nt(a, b):
    return lax.dot_general(a, b, (((1,), (1,)), ((), ())), preferred_element_type=F32)


def _silu(x):
    return x * jax.nn.sigmoid(x)


def _tile(n, pref):
    if n <= pref:
        return n
    t = (pref // LANES) * LANES
    while t >= LANES:
        if n % t == 0:
            return t
        t -= LANES
    raise ValueError("no lane-aligned tile for %d" % n)


def _layout():
    na_w = NA_HEADS * NA_HEAD_DIM
    ret_qk_w = RET_HEADS * RET_QK_DIM
    ret_v_w = RET_HEADS * RET_V_DIM
    diff_qk_w = DIFF_HEADS * 2 * DIFF_QK_DIM
    diff_v_w = DIFF_HEADS * DIFF_V_DIM
    names = ('qa', 'ka', 'va', 'qb', 'kb', 'vb', 'gb', 'qd', 'kd', 'vd', 'gates')
    widths = (na_w, na_w, na_w, ret_qk_w, ret_qk_w, ret_v_w, ret_v_w, diff_qk_w, diff_qk_w, diff_v_w,
              N_BRANCHES * D_MODEL)
    off = {}
    acc = 0
    for n, w in zip(names, widths):
        off[n] = acc
        acc += w
    return off, acc


def _normmod(x, g, shift, scale):
    ms = jnp.mean(x * x, axis=-1, keepdims=True)
    y = x * lax.rsqrt(ms + NORM_EPS) * g
    return y * (1.0 + scale) + shift


def _ada_kernel(c_ref, w_ref, b_ref, o_ref):
    a = _silu(c_ref[...]).astype(BF16)
    o_ref[...] = _dot(a, w_ref[...].astype(BF16)) + b_ref[...]


def _adaln(cond8, w_ada, b_ada):
    d, n = w_ada.shape
    tn = _tile(n, 1024)
    return pl.pallas_call(
        _ada_kernel,
        name="adaln",
        grid=(n // tn,),
        in_specs=[pl.BlockSpec((8, d), lambda j: (0, 0)),
                  pl.BlockSpec((d, tn), lambda j: (0, j)),
                  pl.BlockSpec((1, tn), lambda j: (0, j))],
        out_specs=pl.BlockSpec((8, tn), lambda j: (0, j)),
        out_shape=jax.ShapeDtypeStruct((8, n), F32),
        compiler_params=_params(("parallel",)),
    )(cond8, w_ada, b_ada.reshape(1, n))


def _normmod_mm_kernel(h_ref, mod_ref, g_ref, w_ref, o_ref, u_ref, *, row):
    @pl.when(pl.program_id(1) == 0)
    def _():
        mod = mod_ref[0]
        u = _normmod(h_ref[...], g_ref[...], mod[row:row + 1], mod[row + 1:row + 2])
        u_ref[...] = u.astype(BF16)

    o_ref[...] = _dot(u_ref[...], w_ref[...]).astype(o_ref.dtype)


def _normmod_swiglu_kernel(h_ref, mod_ref, g_ref, w1_ref, w3_ref, o_ref, u_ref, *, row):
    @pl.when(pl.program_id(1) == 0)
    def _():
        mod = mod_ref[0]
        u = _normmod(h_ref[...], g_ref[...], mod[row:row + 1], mod[row + 1:row + 2])
        u_ref[...] = u.astype(BF16)

    u = u_ref[...]
    o_ref[...] = (_silu(_dot(u, w1_ref[...])) * _dot(u, w3_ref[...])).astype(o_ref.dtype)


def _mod_index(tm, seq, batch):
    return lambda i, j: (jnp.minimum((i * tm) // seq, batch), 0, 0)


def _normmod_mm(h, mod, g, w, *, row, n_rows, tm, seq, batch, tn_pref):
    d, n = w.shape
    tn = _tile(n, tn_pref)
    return pl.pallas_call(
        functools.partial(_normmod_mm_kernel, row=row),
        name="norm_in_proj",
        grid=(n_rows // tm, n // tn),
        in_specs=[pl.BlockSpec((tm, d), lambda i, j: (i, 0)),
                  pl.BlockSpec((1, 8, d), _mod_index(tm, seq, batch)),
                  pl.BlockSpec((1, d), lambda i, j: (0, 0)),
                  pl.BlockSpec((d, tn), lambda i, j: (0, j))],
        out_specs=pl.BlockSpec((tm, tn), lambda i, j: (i, j)),
        out_shape=jax.ShapeDtypeStruct((h.shape[0], n), BF16),
        scratch_shapes=[pltpu.VMEM((tm, d), BF16)],
        compiler_params=_params(("parallel", "arbitrary")),
    )(h, mod, g.reshape(1, d), w)


def _normmod_swiglu(h, mod, g, w1, w3, *, row, n_rows, tm, seq, batch):
    d, n = w1.shape
    tn = _tile(n, 512)
    return pl.pallas_call(
        functools.partial(_normmod_swiglu_kernel, row=row),
        name="norm_swiglu_up",
        grid=(n_rows // tm, n // tn),
        in_specs=[pl.BlockSpec((tm, d), lambda i, j: (i, 0)),
                  pl.BlockSpec((1, 8, d), _mod_index(tm, seq, batch)),
                  pl.BlockSpec((1, d), lambda i, j: (0, 0)),
                  pl.BlockSpec((d, tn), lambda i, j: (0, j)),
                  pl.BlockSpec((d, tn), lambda i, j: (0, j))],
        out_specs=pl.BlockSpec((tm, tn), lambda i, j: (i, j)),
        out_shape=jax.ShapeDtypeStruct((h.shape[0], n), BF16),
        scratch_shapes=[pltpu.VMEM((tm, d), BF16)],
        compiler_params=_params(("parallel", "arbitrary")),
    )(h, mod, g.reshape(1, d), w1, w3)


def _mm_res_kernel(a_ref, w_ref, h_ref, mod_ref, o_ref, acc_ref, *, row, nk):
    k = pl.program_id(2)

    @pl.when(k == 0)
    def _():
        acc_ref[...] = jnp.zeros_like(acc_ref)

    acc_ref[...] += _dot(a_ref[...], w_ref[...])

    @pl.when(k == nk - 1)
    def _():
        gate = mod_ref[0][row:row + 1]
        o_ref[...] = h_ref[...] + gate * acc_ref[...]


def _mm_res(a, w, h, mod, *, row, n_rows, tm, seq, batch):
    kdim, n = w.shape
    tn = _tile(n, 1024)
    tk = _tile(kdim, 2048)
    nk = kdim // tk
    mod_idx = _mod_index(tm, seq, batch)
    return pl.pallas_call(
        functools.partial(_mm_res_kernel, row=row, nk=nk),
        name="proj_gated_residual",
        grid=(n_rows // tm, n // tn, nk),
        in_specs=[pl.BlockSpec((tm, tk), lambda i, j, k: (i, k)),
                  pl.BlockSpec((tk, tn), lambda i, j, k: (k, j)),
                  pl.BlockSpec((tm, tn), lambda i, j, k: (i, j)),
                  pl.BlockSpec((1, 8, tn), lambda i, j, k: (mod_idx(i, j)[0], 0, j))],
        out_specs=pl.BlockSpec((tm, tn), lambda i, j, k: (i, j)),
        out_shape=jax.ShapeDtypeStruct(h.shape, F32),
        scratch_shapes=[pltpu.VMEM((tm, tn), F32)],
        compiler_params=_params(("parallel", "parallel", "arbitrary")),
    )(a, w, h, mod)


def _rope_tables(seq):
    t = jnp.arange(seq)
    row = (t // GRID_W).astype(F32)
    col = (t % GRID_W).astype(F32)
    axis_dim = DIFF_QK_DIM // 2
    inv_freq = ROPE_BASE ** (-jnp.arange(0, axis_dim, 2, dtype=F32) / axis_dim)
    ar = row[:, None] * inv_freq
    ac = col[:, None] * inv_freq
    ang = jnp.concatenate([ar, ar, ac, ac], axis=-1)
    reps = LANES // DIFF_QK_DIM
    cos = jnp.tile(jnp.cos(ang), (1, reps))
    sin = jnp.tile(jnp.sin(ang), (1, reps))
    half = DIFF_QK_DIM // 4
    first = (jnp.arange(LANES) % (2 * half)) < half
    sin_dn = jnp.where(first, 0.0, sin)
    sin_up = jnp.where(first, -sin, 0.0)
    return cos, sin_dn, sin_up


def _rope_kernel(q_ref, k_ref, cos_ref, sdn_ref, sup_ref, qo_ref, ko_ref, *, q_scale):
    cos = cos_ref[...]
    sdn = sdn_ref[...]
    sup = sup_ref[...]
    half = DIFF_QK_DIM // 4
    for src, dst, mul in ((q_ref, qo_ref, q_scale), (k_ref, ko_ref, None)):
        for c in range(src.shape[1] // LANES):
            x = src[:, c * LANES:(c + 1) * LANES].astype(F32)
            y = x * cos + pltpu.roll(x, half, 1) * sdn + pltpu.roll(x, LANES - half, 1) * sup
            if mul is not None:
                y = y * mul
            dst[:, c * LANES:(c + 1) * LANES] = y.astype(dst.dtype)


def _rope(p, tabs, *, off, n_lat, seq, tm):
    w = DIFF_HEADS * 2 * DIFF_QK_DIM
    assert off['qd'] % w == 0 and off['kd'] % w == 0
    per = seq // tm
    tab_spec = pl.BlockSpec((tm, LANES), lambda i: (i % per, 0))
    return pl.pallas_call(
        functools.partial(_rope_kernel, q_scale=DIFF_QK_DIM ** -0.5 * math.log2(math.e)),
        name="rope",
        grid=(n_lat // tm,),
        in_specs=[pl.BlockSpec((tm, w), lambda i: (i, off['qd'] // w)),
                  pl.BlockSpec((tm, w), lambda i: (i, off['kd'] // w)),
                  tab_spec, tab_spec, tab_spec],
        out_specs=[pl.BlockSpec((tm, w), lambda i: (i, 0)), pl.BlockSpec((tm, w), lambda i: (i, 0))],
        out_shape=[jax.ShapeDtypeStruct((n_lat, w), BF16), jax.ShapeDtypeStruct((n_lat, w), BF16)],
        compiler_params=_params(("parallel",)),
    )(p, p, *tabs)


def _na_bias_table(rpb, rows):
    del rows
    kh = NA_WIN_H
    cols = jnp.arange(GRID_W)
    c0 = jnp.clip(cols - NA_WIN_W // 2, 0, GRID_W - NA_WIN_W)
    col_in = (cols[None, :] >= c0[:, None]) & (cols[None, :] < c0[:, None] + NA_WIN_W)
    rpb = rpb.astype(F32)
    by_row = jnp.stack([rpb[:, kh - 1 - o:2 * kh - 1 - o, :] for o in range(kh)], axis=1)
    edge = GRID_W - NA_WIN_W
    ext = jnp.concatenate([jnp.repeat(by_row[..., :1], edge, axis=-1), by_row,
                           jnp.repeat(by_row[..., -1:], edge, axis=-1)], axis=-1)
    bias = jnp.stack([ext[..., GRID_W - 1 - q:2 * GRID_W - 1 - q] for q in range(GRID_W)], axis=2)
    bias = jnp.where(col_in[None, None, :, None, :], bias, NEG_INF)
    return bias.reshape(rpb.shape[0], kh, GRID_W, kh * GRID_W)


def _na_kernel(q_ref, k_ref, v_ref, kc_ref, vc_ref, bias_ref, o_ref, *, rb, rows, scale):
    blk = pl.program_id(2)
    kc = kc_ref[...]
    vc = vc_ref[...]
    win = NA_WIN_H * GRID_W
    starts, s_lat, s_ctx = [], [], []
    for j in range(rb):
        r = blk * rb + j
        r0 = jnp.clip(r - NA_WIN_H // 2, 0, rows - NA_WIN_H)
        start = pl.multiple_of(r0 * GRID_W, GRID_W)
        starts.append(start)
        q = q_ref[j * GRID_W:(j + 1) * GRID_W, :]
        s_lat.append(_dot_nt(q, k_ref[pl.ds(start, win), :]) * scale + bias_ref[0, r - r0])
        s_ctx.append(_dot_nt(q, kc) * scale)
    p_lat, p_ctx, dens = [], [], []
    for j in range(rb):
        m = jnp.maximum(jnp.max(s_lat[j], axis=-1, keepdims=True), jnp.max(s_ctx[j], axis=-1, keepdims=True))
        p_l = jnp.exp(s_lat[j] - m)
        p_c = jnp.exp(s_ctx[j] - m)
        dens.append(jnp.sum(p_l, axis=-1, keepdims=True) + jnp.sum(p_c, axis=-1, keepdims=True))
        p_lat.append(p_l.astype(BF16))
        p_ctx.append(p_c.astype(BF16))
    for j in range(rb):
        o = _dot(p_lat[j], v_ref[pl.ds(starts[j], win), :]) + _dot(p_ctx[j], vc)
        o_ref[j * GRID_W:(j + 1) * GRID_W, :] = (o / dens[j]).astype(o_ref.dtype)


def _na_attention(p, bias_tab, *, off, batch, seq, ctx, n_tok):
    rows = seq // GRID_W
    assert rows >= NA_WIN_H
    rb = 8 if rows % 8 == 0 else 1
    dh = NA_HEAD_DIM
    nblk = rows // rb
    qc, kc_, vc_ = off['qa'] // dh, off['ka'] // dh, off['va'] // dh
    ctx_blk0 = batch * seq // ctx
    win = NA_WIN_H * GRID_W
    return pl.pallas_call(
        functools.partial(_na_kernel, rb=rb, rows=rows, scale=dh ** -0.5),
        name="na_attention",
        grid=(batch, NA_HEADS, nblk),
        in_specs=[pl.BlockSpec((rb * GRID_W, dh), lambda b, h, r: (b * nblk + r, qc + h)),
                  pl.BlockSpec((seq, dh), lambda b, h, r: (b, kc_ + h)),
                  pl.BlockSpec((seq, dh), lambda b, h, r: (b, vc_ + h)),
                  pl.BlockSpec((ctx, dh), lambda b, h, r: (ctx_blk0 + b, kc_ + h)),
                  pl.BlockSpec((ctx, dh), lambda b, h, r: (ctx_blk0 + b, vc_ + h)),
                  pl.BlockSpec((1, NA_WIN_H, GRID_W, win), lambda b, h, r: (h, 0, 0, 0))],
        out_specs=pl.BlockSpec((rb * GRID_W, dh), lambda b, h, r: (b * nblk + r, h)),
        out_shape=jax.ShapeDtypeStruct((n_tok, NA_HEADS * dh), BF16),
        compiler_params=_params(("parallel", "parallel", "arbitrary")),
    )(p, p, p, p, p, bias_tab)


def _ctx_na_kernel(q_ref, k_ref, v_ref, prev_ref, o_ref, *, scale):
    del prev_ref
    s = _dot_nt(q_ref[...], k_ref[...]) * scale
    m = jnp.max(s, axis=-1, keepdims=True)
    e = jnp.exp(s - m)
    den = jnp.sum(e, axis=-1, keepdims=True)
    o_ref[...] = (_dot(e.astype(BF16), v_ref[...]) / den).astype(o_ref.dtype)


def _ctx_na_attention(p, oa, *, off, batch, seq, ctx):
    dh = NA_HEAD_DIM
    qc, kc_, vc_ = off['qa'] // dh, off['ka'] // dh, off['va'] // dh
    blk0 = batch * seq // ctx
    return pl.pallas_call(
        functools.partial(_ctx_na_kernel, scale=dh ** -0.5),
        name="ctx_na_attention",
        grid=(batch, NA_HEADS),
        in_specs=[pl.BlockSpec((ctx, dh), lambda b, h: (blk0 + b, qc + h)),
                  pl.BlockSpec((ctx, dh), lambda b, h: (blk0 + b, kc_ + h)),
                  pl.BlockSpec((ctx, dh), lambda b, h: (blk0 + b, vc_ + h)),
                  pl.BlockSpec(memory_space=pl.ANY)],
        out_specs=pl.BlockSpec((ctx, dh), lambda b, h: (blk0 + b, h)),
        out_shape=jax.ShapeDtypeStruct(oa.shape, oa.dtype),
        input_output_aliases={3: 0},
        compiler_params=_params(("parallel", "parallel")),
    )(p, p, p, oa)


def _split_maps(q):
    lane = lax.broadcasted_iota(jnp.int32, q.shape, 1)
    zero = jnp.zeros_like(q)
    return jnp.concatenate([jnp.where(lane < DIFF_QK_DIM, q, zero), jnp.where(lane >= DIFF_QK_DIM, q, zero)],
                           axis=0)


def _diff_finish(o1, o2, par_ref):
    lam = par_ref[1:2, :]
    o = o1 - lam * o2
    y = o * lax.rsqrt(jnp.mean(o * o, axis=-1, keepdims=True) + SUBLN_EPS)
    return y * par_ref[0:1, :]


ONES_ROWS = 16


def _diff_kernel(q_ref, k_ref, v_ref, kc_ref, vc_ref, par_ref, o_ref, vt_ref, vct_ref, s_ref, *, tk):
    tq = q_ref.shape[0]
    dv = v_ref.shape[1]
    n_chunks = k_ref.shape[0] // tk

    @pl.when(pl.program_id(2) == 0)
    def _():
        for c in range(n_chunks):
            vt_ref[c, 0:dv, :] = v_ref[c * tk:(c + 1) * tk, :].astype(F32).T.astype(BF16)
            vt_ref[c, dv:, :] = jnp.ones((ONES_ROWS, tk), BF16)
        vct_ref[0:dv, :] = vc_ref[...].astype(F32).T.astype(BF16)
        vct_ref[dv:, :] = jnp.ones((ONES_ROWS, vct_ref.shape[1]), BF16)

    q_t = q_ref[...].astype(F32).T
    row = lax.broadcasted_iota(jnp.int32, q_t.shape, 0)
    qm = jnp.concatenate([jnp.where(row < DIFF_QK_DIM, q_t, 0.0), jnp.where(row >= DIFF_QK_DIM, q_t, 0.0)],
                         axis=1).astype(BF16)

    def scores(c):
        start = pl.multiple_of(c * tk, tk)
        return _dot(k_ref[pl.ds(start, tk), :], qm)

    def consume(s, vtb, carry):
        m, acc = carry
        m_new = jnp.maximum(m, jnp.max(s, axis=0, keepdims=True))
        alpha = jnp.exp2(m - m_new)
        e = jnp.exp2(s - m_new).astype(BF16)
        return m_new, alpha * acc + _dot(vtb, e)

    def body(j, carry):
        c0 = 2 * j
        s_ref[1] = scores(c0 + 1)
        carry = consume(s_ref[0], vt_ref[c0], carry)
        s_ref[0] = scores(jnp.minimum(c0 + 2, n_chunks - 1))
        return consume(s_ref[1], vt_ref[c0 + 1], carry)

    init = (jnp.full((1, 2 * tq), -jnp.inf, F32), jnp.zeros((dv + ONES_ROWS, 2 * tq), F32))
    s_ref[0] = scores(0)
    carry = lax.fori_loop(0, n_chunks // 2, body, init)
    m, acc = consume(_dot(kc_ref[...], qm), vct_ref[...], carry)
    o = acc[0:dv] / acc[dv:dv + 1]
    o_ref[...] = _diff_finish(o[:, :tq].T, o[:, tq:].T, par_ref).astype(o_ref.dtype)


def _diff_attention(qr, kr, p, par, *, off, batch, seq, ctx, n_tok):
    dq2 = 2 * DIFF_QK_DIM
    dv = DIFF_V_DIM
    assert dq2 == LANES and dv == LANES
    tq = _tile(seq, 256)
    tk = _tile(seq, 512)
    assert (seq // tk) % 2 == 0
    nq = seq // tq
    kdc, vdc = off['kd'] // dq2, off['vd'] // dv
    ctx_blk0 = batch * seq // ctx
    return pl.pallas_call(
        functools.partial(_diff_kernel, tk=tk),
        name="diff_attention",
        scratch_shapes=[pltpu.VMEM((seq // tk, dv + ONES_ROWS, tk), BF16), pltpu.VMEM((dv + ONES_ROWS, ctx), BF16),
                        pltpu.VMEM((2, tk, 2 * tq), F32)],
        grid=(batch, DIFF_HEADS, nq),
        in_specs=[pl.BlockSpec((tq, dq2), lambda b, h, i: (b * nq + i, h)),
                  pl.BlockSpec((seq, dq2), lambda b, h, i: (b, h)),
                  pl.BlockSpec((seq, dv), lambda b, h, i: (b, vdc + h)),
                  pl.BlockSpec((ctx, dq2), lambda b, h, i: (ctx_blk0 + b, kdc + h)),
                  pl.BlockSpec((ctx, dv), lambda b, h, i: (ctx_blk0 + b, vdc + h)),
                  pl.BlockSpec((8, dv), lambda b, h, i: (0, 0))],
        out_specs=pl.BlockSpec((tq, dv), lambda b, h, i: (b * nq + i, h)),
        out_shape=jax.ShapeDtypeStruct((n_tok, DIFF_HEADS * dv), BF16),
        compiler_params=_params(("parallel", "parallel", "arbitrary")),
    )(qr, kr, p, p, p, par)


def _ctx_diff_kernel(q_ref, k_ref, v_ref, par_ref, prev_ref, o_ref, *, scale):
    del prev_ref
    n = q_ref.shape[0]
    s = _dot_nt(_split_maps(q_ref[...]), k_ref[...]) * scale
    m = jnp.max(s, axis=-1, keepdims=True)
    e = jnp.exp(s - m)
    den = jnp.sum(e, axis=-1, keepdims=True)
    o = _dot(e.astype(BF16), v_ref[...]) / den
    o_ref[...] = _diff_finish(o[:n], o[n:], par_ref).astype(o_ref.dtype)


def _ctx_diff_attention(p, od, par, *, off, batch, seq, ctx):
    dq2 = 2 * DIFF_QK_DIM
    dv = DIFF_V_DIM
    qdc, kdc, vdc = off['qd'] // dq2, off['kd'] // dq2, off['vd'] // dv
    blk0 = batch * seq // ctx
    return pl.pallas_call(
        functools.partial(_ctx_diff_kernel, scale=DIFF_QK_DIM ** -0.5),
        name="ctx_diff_attention",
        grid=(batch, DIFF_HEADS),
        in_specs=[pl.BlockSpec((ctx, dq2), lambda b, h: (blk0 + b, qdc + h)),
                  pl.BlockSpec((ctx, dq2), lambda b, h: (blk0 + b, kdc + h)),
                  pl.BlockSpec((ctx, dv), lambda b, h: (blk0 + b, vdc + h)),
                  pl.BlockSpec((8, dv), lambda b, h: (0, 0)),
                  pl.BlockSpec(memory_space=pl.ANY)],
        out_specs=pl.BlockSpec((ctx, dv), lambda b, h: (blk0 + b, h)),
        out_shape=jax.ShapeDtypeStruct(od.shape, od.dtype),
        input_output_aliases={4: 0},
        compiler_params=_params(("parallel", "parallel")),
    )(p, p, p, par, od)


def _ret_tables(log_gamma, reverse):
    cs = RET_CHUNK
    k_scale = RET_QK_DIM ** -0.5
    pos = jnp.arange(cs, dtype=F32)
    lg = log_gamma[:, None]
    rel = pos[:, None] - pos[None, :]
    if reverse:
        rel = -rel
        q_decay = jnp.exp(lg * (cs - pos))
        k_decay = jnp.exp(lg * pos)
    else:
        q_decay = jnp.exp(lg * (pos + 1.0))
        k_decay = jnp.exp(lg * (cs - 1.0 - pos))
    intra = jnp.where(rel >= 0, jnp.exp(lg[:, :, None] * jnp.maximum(rel, 0.0)), 0.0) * k_scale
    chunk_decay = jnp.broadcast_to(jnp.exp(lg * cs)[:, :, None], (RET_HEADS, 1, RET_V_DIM))
    return intra, q_decay[:, :, None], (k_decay * k_scale)[:, :, None], chunk_decay


def _ret_kernel(*refs, nh, final):
    q_refs, k_refs, v_refs = refs[0:nh], refs[nh:2 * nh], refs[2 * nh:3 * nh]
    pos = 3 * nh
    if final:
        g_refs = refs[pos:pos + nh]
        of_ref, gain_ref = refs[pos + nh:pos + nh + 2]
        pos += nh + 2
    intra_ref, qdec_ref, kdec_ref, cd_ref, o_ref, s_ref = refs[pos:pos + 6]
    dv = RET_V_DIM

    @pl.when(pl.program_id(1) == 0)
    def _():
        s_ref[...] = jnp.zeros_like(s_ref)

    atts, outs = [], []
    for h in range(nh):
        atts.append((_dot_nt(q_refs[h][...], k_refs[h][...]) * intra_ref[h]).astype(BF16))
    for h in range(nh):
        qd = (q_refs[h][...].astype(F32) * qdec_ref[h]).astype(BF16)
        outs.append(_dot(atts[h], v_refs[h][...]) + _dot(qd, s_ref[h].astype(BF16)))
    for h in range(nh):
        kd_t = (k_refs[h][...].astype(F32) * kdec_ref[h]).T.astype(BF16)
        s_ref[h] = s_ref[h] * cd_ref[h] + _dot(kd_t, v_refs[h][...])
    for h in range(nh):
        o = outs[h]
        cols = slice(h * dv, (h + 1) * dv)
        if final:
            o = o + of_ref[:, cols]
            mu = jnp.mean(o, axis=-1, keepdims=True)
            var = jnp.mean(jnp.square(o - mu), axis=-1, keepdims=True)
            y = (o - mu) * lax.rsqrt(var + NORM_EPS) * gain_ref[:, cols]
            o_ref[:, cols] = (_silu(g_refs[h][...].astype(F32)) * y).astype(o_ref.dtype)
        else:
            o_ref[:, cols] = o


def _retention(p, tabs_f, tabs_b, gain, *, off, batch, seq, ctx, n_tok):
    cs, nh, dk, dv = RET_CHUNK, RET_HEADS, RET_QK_DIM, RET_V_DIM
    lc, sc = ctx // cs, seq // cs
    ctx_blk0 = batch * seq // cs
    vw = nh * dv
    assert off['qb'] % dk == 0 and off['kb'] % dk == 0 and off['vb'] % dv == 0 and off['gb'] % dv == 0

    def chunk_fwd(b, t):
        return jnp.where(t < lc, ctx_blk0 + b * lc + t, b * sc + (t - lc))

    def chunk_bwd(b, t):
        return jnp.where(t < lc, ctx_blk0 + b * lc + (lc - 1 - t), b * sc + (sc - 1 - (t - lc)))

    def run(chunk, tabs, final, of):
        def head_spec(width, col0, h):
            return pl.BlockSpec((cs, width), lambda b, t: (chunk(b, t), col0 + h))

        in_specs = ([head_spec(dk, off['qb'] // dk, h) for h in range(nh)]
                    + [head_spec(dk, off['kb'] // dk, h) for h in range(nh)]
                    + [head_spec(dv, off['vb'] // dv, h) for h in range(nh)])
        args = [p] * (3 * nh)
        if final:
            in_specs += [head_spec(dv, off['gb'] // dv, h) for h in range(nh)]
            in_specs += [pl.BlockSpec((cs, vw), lambda b, t: (chunk(b, t), 0)),
                         pl.BlockSpec((1, vw), lambda b, t: (0, 0))]
            args += [p] * nh + [of, gain.reshape(1, vw).astype(F32)]
        in_specs += [pl.BlockSpec((nh, cs, cs), lambda b, t: (0, 0, 0)),
                     pl.BlockSpec((nh, cs, 1), lambda b, t: (0, 0, 0)),
                     pl.BlockSpec((nh, cs, 1), lambda b, t: (0, 0, 0)),
                     pl.BlockSpec((nh, 1, dv), lambda b, t: (0, 0, 0))]
        args += list(tabs)
        return pl.pallas_call(
            functools.partial(_ret_kernel, nh=nh, final=final),
            name="retention_bwd_norm_gate" if final else "retention_fwd",
            grid=(batch, lc + sc),
            in_specs=in_specs,
            out_specs=pl.BlockSpec((cs, vw), lambda b, t: (chunk(b, t), 0)),
            out_shape=jax.ShapeDtypeStruct((n_tok, vw), BF16 if final else F32),
            scratch_shapes=[pltpu.VMEM((nh, dk, dv), F32)],
            compiler_params=_params(("parallel", "arbitrary")),
        )(*args)

    o_f = run(chunk_fwd, tabs_f, False, None)
    return run(chunk_bwd, tabs_b, True, o_f)


def _merge_kernel(oa_ref, ob_ref, od_ref, ga_ref, gb_ref, gd_ref, wa_ref, wb_ref, wd_ref, o_ref):
    y = jax.nn.sigmoid(ga_ref[...].astype(F32)) * _dot(oa_ref[...], wa_ref[...])
    y += jax.nn.sigmoid(gb_ref[...].astype(F32)) * _dot(ob_ref[...], wb_ref[...])
    y += jax.nn.sigmoid(gd_ref[...].astype(F32)) * _dot(od_ref[...], wd_ref[...])
    o_ref[...] = y.astype(o_ref.dtype)


def _merge(oa, ob, od, p, wa, wb, wd, *, off, n_rows, tm):
    d = D_MODEL
    tn = _tile(d, 512)
    assert off['gates'] % tn == 0
    g0 = off['gates'] // tn
    nj = d // tn

    def full(a):
        return pl.BlockSpec((tm, a.shape[1]), lambda i, j: (i, 0))

    def wspec(w):
        return pl.BlockSpec((w.shape[0], tn), lambda i, j: (0, j))

    def gspec(br):
        return pl.BlockSpec((tm, tn), lambda i, j: (i, g0 + br * nj + j))

    return pl.pallas_call(
        _merge_kernel,
        name="branch_merge",
        grid=(n_rows // tm, nj),
        in_specs=[full(oa), full(ob), full(od), gspec(0), gspec(1), gspec(2), wspec(wa), wspec(wb), wspec(wd)],
        out_specs=pl.BlockSpec((tm, tn), lambda i, j: (i, j)),
        out_shape=jax.ShapeDtypeStruct((oa.shape[0], d), BF16),
        compiler_params=_params(("parallel", "arbitrary")),
    )(oa, ob, od, p, p, p, wa, wb, wd)


def _router_kernel(h_ref, mod_ref, g_ref, rw_ref, rb_ref, u_ref, idx_ref, wt_ref, *, row):
    mod = mod_ref[0]
    u = _normmod(h_ref[...], g_ref[...], mod[row:row + 1], mod[row + 1:row + 2])
    u_hi = u.astype(BF16)
    u_ref[...] = u_hi
    u_lo = (u - u_hi.astype(F32)).astype(BF16)
    w = rw_ref[...]
    w_hi = w.astype(BF16)
    w_lo = (w - w_hi.astype(F32)).astype(BF16)
    logits = _dot(u_hi, w_hi) + _dot(u_hi, w_lo) + _dot(u_lo, w_hi) + rb_ref[...]
    lane = lax.broadcasted_iota(jnp.int32, logits.shape, 1)
    lg = jnp.where(lane < N_EXPERTS, logits, -jnp.inf)
    v1 = jnp.max(lg, axis=-1, keepdims=True)
    lane_f = lane.astype(F32)
    i1 = jnp.min(jnp.where(lg == v1, lane_f, float(LANES)), axis=-1, keepdims=True).astype(jnp.int32)
    lg2 = jnp.where(lane == i1, -jnp.inf, lg)
    v2 = jnp.max(lg2, axis=-1, keepdims=True)
    i2 = jnp.min(jnp.where(lg2 == v2, lane_f, float(LANES)), axis=-1, keepdims=True).astype(jnp.int32)
    e = jnp.exp(v2 - v1)
    w1 = 1.0 / (1.0 + e)
    w2 = e / (1.0 + e)
    idx_ref[...] = jnp.where(lane == 0, i1, jnp.where(lane == 1, i2, 0))
    wt_ref[...] = jnp.where(lane == 0, w1, jnp.where(lane == 1, w2, 0.0))


def _router(h, mod, g, router_w, router_b, *, row, n_rows, tm, seq, batch):
    d = D_MODEL
    rw = jnp.zeros((d, LANES), F32).at[:, :N_EXPERTS].set(router_w.astype(F32))
    rb = jnp.zeros((1, LANES), F32).at[0, :N_EXPERTS].set(router_b.astype(F32))
    mod_idx = _mod_index(tm, seq, batch)
    return pl.pallas_call(
        functools.partial(_router_kernel, row=row),
        name="norm_router",
        grid=(n_rows // tm,),
        in_specs=[pl.BlockSpec((tm, d), lambda i: (i, 0)),
                  pl.BlockSpec((1, 8, d), lambda i: mod_idx(i, 0)),
                  pl.BlockSpec((1, d), lambda i: (0, 0)),
                  pl.BlockSpec((d, LANES), lambda i: (0, 0)),
                  pl.BlockSpec((1, LANES), lambda i: (0, 0))],
        out_specs=[pl.BlockSpec((tm, d), lambda i: (i, 0)),
                   pl.BlockSpec((tm, LANES), lambda i: (i, 0)),
                   pl.BlockSpec((tm, LANES), lambda i: (i, 0))],
        out_shape=[jax.ShapeDtypeStruct((n_rows, d), BF16),
                   jax.ShapeDtypeStruct((n_rows, LANES), jnp.int32),
                   jax.ShapeDtypeStruct((n_rows, LANES), F32)],
        compiler_params=_params(("parallel",)),
    )(h, mod, g.reshape(1, d), rw, rb)


def _route_plan(ridx, rwt, tme):
    n_tok = ridx.shape[0]
    n2 = n_tok * TOP_K
    n_pad = n2 + N_EXPERTS * tme
    n_tiles = n_pad // tme
    e_flat = ridx.reshape(-1)
    order = jnp.argsort(e_flat, stable=True).astype(jnp.int32)
    counts = jnp.sum(e_flat[:, None] == jnp.arange(N_EXPERTS)[None, :], axis=0).astype(jnp.int32)
    padded = ((counts + tme - 1) // tme) * tme
    pend = jnp.cumsum(padded)
    pstart = pend - padded
    ustart = jnp.cumsum(counts) - counts
    tile_start = jnp.arange(n_tiles, dtype=jnp.int32) * tme
    tile_expert = jnp.minimum(jnp.searchsorted(pend, tile_start, side='right'), N_EXPERTS - 1).astype(jnp.int32)
    tile_valid = (tile_start < pend[-1]).astype(jnp.int32)
    slot = jnp.arange(n_pad, dtype=jnp.int32)
    slot_e = jnp.repeat(tile_expert, tme)
    within = slot - pstart[slot_e]
    valid = (within < counts[slot_e]) & (jnp.repeat(tile_valid, tme) > 0)
    src = jnp.clip(ustart[slot_e] + within, 0, n2 - 1)
    assign = order[src]
    perm_tok = jnp.where(valid, assign // TOP_K, 0)
    w_sorted = jnp.where(valid, rwt.reshape(-1)[assign], 0.0)
    e_sorted = e_flat[order]
    pos = jnp.arange(n2, dtype=jnp.int32) - ustart[e_sorted] + pstart[e_sorted]
    slot_of = pos[jnp.argsort(order)].reshape(n_tok, TOP_K)
    return perm_tok, w_sorted, slot_of, tile_expert, tile_valid


def _moe_up_kernel(te_ref, tv_ref, a_ref, w1_ref, w3_ref, o_ref):
    i = pl.program_id(1)

    @pl.when(tv_ref[i] > 0)
    def _():
        a = a_ref[...]
        o_ref[...] = (_silu(_dot(a, w1_ref[0])) * _dot(a, w3_ref[0])).astype(o_ref.dtype)

    @pl.when(tv_ref[i] == 0)
    def _():
        o_ref[...] = jnp.zeros_like(o_ref)


def _moe_up(a, w1, w3, tile_expert, tile_valid, *, tme):
    n_pad, d = a.shape
    f = w1.shape[2]
    tn = _tile(f, 1024)
    return pl.pallas_call(
        _moe_up_kernel,
        name="moe_up",
        grid_spec=pltpu.PrefetchScalarGridSpec(
            num_scalar_prefetch=2,
            grid=(f // tn, n_pad // tme),
            in_specs=[pl.BlockSpec((tme, d), lambda j, i, te, tv: (i, 0)),
                      pl.BlockSpec((1, d, tn), lambda j, i, te, tv: (te[i], 0, j)),
                      pl.BlockSpec((1, d, tn), lambda j, i, te, tv: (te[i], 0, j))],
            out_specs=pl.BlockSpec((tme, tn), lambda j, i, te, tv: (i, j))),
        out_shape=jax.ShapeDtypeStruct((n_pad, f), BF16),
        compiler_params=_params(("parallel", "arbitrary")),
    )(tile_expert, tile_valid, a, w1, w3)


def _moe_down_kernel(te_ref, tv_ref, a_ref, w_ref, ws_ref, o_ref, acc_ref, *, nk):
    i = pl.program_id(0)
    k = pl.program_id(1)

    @pl.when(k == 0)
    def _():
        acc_ref[...] = jnp.zeros_like(acc_ref)

    @pl.when(tv_ref[i] > 0)
    def _():
        acc_ref[...] += _dot(a_ref[...], w_ref[0])

    @pl.when(k == nk - 1)
    def _():
        o_ref[...] = (ws_ref[...] * acc_ref[...]).astype(o_ref.dtype)


def _moe_down(a, w2, w_sorted, tile_expert, tile_valid, *, tme):
    n_pad, f = a.shape
    d = w2.shape[2]
    tk = _tile(f, 1024)
    nk = f // tk
    return pl.pallas_call(
        functools.partial(_moe_down_kernel, nk=nk),
        name="moe_down",
        grid_spec=pltpu.PrefetchScalarGridSpec(
            num_scalar_prefetch=2,
            grid=(n_pad // tme, nk),
            in_specs=[pl.BlockSpec((tme, tk), lambda i, k, te, tv: (i, k)),
                      pl.BlockSpec((1, tk, d), lambda i, k, te, tv: (te[i], k, 0)),
                      pl.BlockSpec((tme, 1), lambda i, k, te, tv: (i, 0))],
            out_specs=pl.BlockSpec((tme, d), lambda i, k, te, tv: (i, 0)),
            scratch_shapes=[pltpu.VMEM((tme, d), F32)]),
        out_shape=jax.ShapeDtypeStruct((n_pad, d), BF16),
        compiler_params=_params(("parallel", "arbitrary")),
    )(tile_expert, tile_valid, a, w2, w_sorted.reshape(n_pad, 1))


def _combine_norm_kernel(h_ref, y1_ref, y2_ref, mod_ref, g_ref, o_ref, *, row):
    gate = mod_ref[0][row:row + 1]
    x = h_ref[...] + gate * (y1_ref[...].astype(F32) + y2_ref[...].astype(F32))
    ms = jnp.mean(x * x, axis=-1, keepdims=True)
    o_ref[...] = x * lax.rsqrt(ms + NORM_EPS) * g_ref[...]


def _combine_norm(h, y12, mod, g, *, row, n_rows, tm, seq, batch):
    d = D_MODEL
    mod_idx = _mod_index(tm, seq, batch)
    rowspec = pl.BlockSpec((tm, d), lambda i: (i, 0))
    return pl.pallas_call(
        functools.partial(_combine_norm_kernel, row=row),
        name="moe_combine_final_norm",
        grid=(n_rows // tm,),
        in_specs=[rowspec, rowspec, pl.BlockSpec((tm, d), lambda i: (i, 1)),
                  pl.BlockSpec((1, 8, d), lambda i: mod_idx(i, 0)),
                  pl.BlockSpec((1, d), lambda i: (0, 0))],
        out_specs=rowspec,
        out_shape=jax.ShapeDtypeStruct((n_rows, d), F32),
        compiler_params=_params(("parallel",)),
    )(h, y12, y12, mod, g.reshape(1, d))


def _token_mixers(h, mod, lp, layer_idx, rope_tabs, *, dims, need_ctx):
    batch, seq, ctx, n_tok, n_lat, tm = dims
    off, in_w = _layout()
    bf = lambda a: a.astype(BF16)
    p = _normmod_mm(h, mod, lp['norm1'], bf(lp['w_in']), row=0, n_rows=n_tok, tm=tm, seq=seq, batch=batch,
                    tn_pref=768)

    bias_tab = _na_bias_table(lp['na_rpb'], seq // GRID_W)
    oa = _na_attention(p, bias_tab, off=off, batch=batch, seq=seq, ctx=ctx, n_tok=n_tok)

    log_gamma = jax.nn.log_sigmoid(lp['ret_decay'].astype(F32))
    ob = _retention(p, _ret_tables(log_gamma[0], False), _ret_tables(log_gamma[1], True), lp['ret_gn'],
                    off=off, batch=batch, seq=seq, ctx=ctx, n_tok=n_tok)

    lam_init = 0.8 - 0.6 * math.exp(-0.3 * layer_idx)
    lq1, lk1, lq2, lk2 = lp['diff_lam'].astype(F32)
    lam = jnp.exp(jnp.sum(lq1 * lk1)) - jnp.exp(jnp.sum(lq2 * lk2)) + lam_init
    par = jnp.zeros((8, DIFF_V_DIM), F32)
    par = par.at[0].set(lp['diff_subln'].astype(F32) * (1.0 - lam_init)).at[1].set(lam)
    qr, kr = _rope(p, rope_tabs, off=off, n_lat=n_lat, seq=seq, tm=tm)
    od = _diff_attention(qr, kr, p, par, off=off, batch=batch, seq=seq, ctx=ctx, n_tok=n_tok)

    if need_ctx:
        oa = _ctx_na_attention(p, oa, off=off, batch=batch, seq=seq, ctx=ctx)
        od = _ctx_diff_attention(p, od, par, off=off, batch=batch, seq=seq, ctx=ctx)
    n_rows = n_tok if need_ctx else n_lat
    ymid = _merge(oa, ob, od, p, bf(lp['w_br_a']), bf(lp['w_br_b']), bf(lp['w_br_c']), off=off, n_rows=n_rows, tm=tm)
    return _mm_res(ymid, bf(lp['w_out']), h, mod, row=2, n_rows=n_rows, tm=tm, seq=seq, batch=batch)


def _forward(x, c, ctx_tok, c_ctx, layers, final_norm):
    batch, seq, d = x.shape
    ctx = ctx_tok.shape[1]
    n_lat = batch * seq
    n_tok = n_lat + batch * ctx
    tm = min(ROW_TILE, batch * ctx)
    assert d == D_MODEL and seq % tm == 0 and (batch * ctx) % tm == 0 and seq % GRID_W == 0
    assert seq % RET_CHUNK == 0 and ctx % RET_CHUNK == 0 and n_lat % ctx == 0 and batch + 1 <= 8
    dims = (batch, seq, ctx, n_tok, n_lat, tm)
    bf = lambda a: a.astype(BF16)

    h = jnp.concatenate([x.reshape(n_lat, d), ctx_tok.reshape(batch * ctx, d)], axis=0).astype(F32)
    cond8 = jnp.zeros((8, d), F32).at[:batch].set(c).at[batch].set(c_ctx)
    rope_tabs = _rope_tables(seq)
    n_layers = len(layers)
    out = None
    for li, lp in enumerate(layers):
        need_ctx = li < n_layers - 1
        m = _adaln(cond8, lp['w_ada'], lp['b_ada'])
        mod = jnp.zeros((batch + 1, 8, d), F32).at[:, :6].set(m[:batch + 1].reshape(batch + 1, 6, d))
        h = _token_mixers(h, mod, lp, li, rope_tabs, dims=dims, need_ctx=need_ctx)
        n_rows = n_tok if need_ctx else n_lat
        if 'ffn_w1' in lp:
            a = _normmod_swiglu(h, mod, lp['norm2'], bf(lp['ffn_w1']), bf(lp['ffn_w3']), row=3, n_rows=n_rows,
                                tm=tm, seq=seq, batch=batch)
            h = _mm_res(a, bf(lp['ffn_w2']), h, mod, row=5, n_rows=n_rows, tm=tm, seq=seq, batch=batch)
        else:
            assert not need_ctx and li == n_layers - 1
            tme = min(MOE_ROW_TILE, n_lat)
            u, ridx, rwt = _router(h, mod, lp['norm2'], lp['router_w'], lp['router_b'], row=3, n_rows=n_lat,
                                   tm=tm, seq=seq, batch=batch)
            perm_tok, w_sorted, slot_of, tile_expert, tile_valid = _route_plan(ridx[:, :TOP_K], rwt[:, :TOP_K], tme)
            a = _moe_up(jnp.take(u, perm_tok, axis=0), bf(lp['exp_w1']), bf(lp['exp_w3']), tile_expert, tile_valid,
                        tme=tme)
            y = _moe_down(a, bf(lp['exp_w2']), w_sorted, tile_expert, tile_valid, tme=tme)
            y12 = jnp.take(y, slot_of.reshape(-1), axis=0).reshape(n_lat, TOP_K * d)
            out = _combine_norm(h, y12, mod, final_norm, row=5, n_rows=n_lat, tm=tm, seq=seq, batch=batch)
    return out.reshape(batch, seq, d)


def kernel(x, c, ctx, c_ctx, l0_w_ada, l0_b_ada, l0_norm1, l0_w_in, l0_na_rpb, l0_ret_decay, l0_ret_gn, l0_diff_lam, l0_diff_subln, l0_w_br_a, l0_w_br_b, l0_w_br_c, l0_w_out, l0_norm2, l0_ffn_w1, l0_ffn_w3, l0_ffn_w2, l1_w_ada, l1_b_ada, l1_norm1, l1_w_in, l1_na_rpb, l1_ret_decay, l1_ret_gn, l1_diff_lam, l1_diff_subln, l1_w_br_a, l1_w_br_b, l1_w_br_c, l1_w_out, l1_norm2, l1_router_w, l1_router_b, l1_exp_w1, l1_exp_w3, l1_exp_w2, final_norm):
    layers = (
        dict(w_ada=l0_w_ada, b_ada=l0_b_ada, norm1=l0_norm1, w_in=l0_w_in, na_rpb=l0_na_rpb,
             ret_decay=l0_ret_decay, ret_gn=l0_ret_gn, diff_lam=l0_diff_lam, diff_subln=l0_diff_subln,
             w_br_a=l0_w_br_a, w_br_b=l0_w_br_b, w_br_c=l0_w_br_c, w_out=l0_w_out, norm2=l0_norm2,
             ffn_w1=l0_ffn_w1, ffn_w3=l0_ffn_w3, ffn_w2=l0_ffn_w2),
        dict(w_ada=l1_w_ada, b_ada=l1_b_ada, norm1=l1_norm1, w_in=l1_w_in, na_rpb=l1_na_rpb,
             ret_decay=l1_ret_decay, ret_gn=l1_ret_gn, diff_lam=l1_diff_lam, diff_subln=l1_diff_subln,
             w_br_a=l1_w_br_a, w_br_b=l1_w_br_b, w_br_c=l1_w_br_c, w_out=l1_w_out, norm2=l1_norm2,
             router_w=l1_router_w, router_b=l1_router_b, exp_w1=l1_exp_w1, exp_w3=l1_exp_w3,
             exp_w2=l1_exp_w2),
    )
    return _forward(x, c, ctx, c_ctx, layers, final_norm)
```

```python
import functools
import math

import jax
import jax.numpy as jnp
from jax import lax
from jax.experimental import pallas as pl
from jax.experimental.pallas import tpu as pltpu

D_MODEL = 2048
GRID_W = 64
NA_HEADS = 6
NA_HEAD_DIM = 128
NA_WIN_H = 8
NA_WIN_W = 16
RET_HEADS = 4
RET_QK_DIM = 128
RET_V_DIM = 256
RET_CHUNK = 128
DIFF_HEADS = 6
DIFF_QK_DIM = 64
DIFF_V_DIM = 128
ROPE_BASE = 10000.0
N_EXPERTS = 8
TOP_K = 2
N_BRANCHES = 3
NORM_EPS = 1e-6
SUBLN_EPS = 1e-5
NEG_INF = -1e30

LANES = 128
ROW_TILE = 512
MOE_ROW_TILE = 512
VMEM_LIMIT = 56 << 20

F32 = jnp.float32
BF16 = jnp.bfloat16


def _params(sem, vmem=VMEM_LIMIT):
    return pltpu.CompilerParams(dimension_semantics=sem, vmem_limit_bytes=vmem)


def _dot(a, b):
    return jnp.dot(a, b, preferred_element_type=F32)


def _dot_nt(a, b):
    return lax.dot_general(a, b, (((1,), (1,)), ((), ())), preferred_element_type=F32)


def _silu(x):
    return x * jax.nn.sigmoid(x)


def _tile(n, pref):
    if n <= pref:
        return n
    t = (pref // LANES) * LANES
    while t >= LANES:
        if n % t == 0:
            return t
        t -= LANES
    raise ValueError("no lane-aligned tile for %d" % n)


def _layout():
    na_w = NA_HEADS * NA_HEAD_DIM
    ret_qk_w = RET_HEADS * RET_QK_DIM
    ret_v_w = RET_HEADS * RET_V_DIM
    diff_qk_w = DIFF_HEADS * 2 * DIFF_QK_DIM
    diff_v_w = DIFF_HEADS * DIFF_V_DIM
    names = ('qa', 'ka', 'va', 'qb', 'kb', 'vb', 'gb', 'qd', 'kd', 'vd', 'gates')
    widths = (na_w, na_w, na_w, ret_qk_w, ret_qk_w, ret_v_w, ret_v_w, diff_qk_w, diff_qk_w, diff_v_w,
              N_BRANCHES * D_MODEL)
    off = {}
    acc = 0
    for n, w in zip(names, widths):
        off[n] = acc
        acc += w
    return off, acc


def _normmod(x, g, shift, scale):
    ms = jnp.mean(x * x, axis=-1, keepdims=True)
    y = x * lax.rsqrt(ms + NORM_EPS) * g
    return y * (1.0 + scale) + shift


def _ada_kernel(c_ref, w_ref, b_ref, o_ref):
    a = _silu(c_ref[...]).astype(BF16)
    o_ref[...] = _dot(a, w_ref[...].astype(BF16)) + b_ref[...]


def _adaln(cond8, w_ada, b_ada):
    d, n = w_ada.shape
    tn = _tile(n, 1024)
    return pl.pallas_call(
        _ada_kernel,
        name="adaln",
        grid=(n // tn,),
        in_specs=[pl.BlockSpec((8, d), lambda j: (0, 0)),
                  pl.BlockSpec((d, tn), lambda j: (0, j)),
                  pl.BlockSpec((1, tn), lambda j: (0, j))],
        out_specs=pl.BlockSpec((8, tn), lambda j: (0, j)),
        out_shape=jax.ShapeDtypeStruct((8, n), F32),
        compiler_params=_params(("parallel",)),
    )(cond8, w_ada, b_ada.reshape(1, n))


def _normmod_mm_kernel(h_ref, mod_ref, g_ref, w_ref, o_ref, u_ref, *, row):
    @pl.when(pl.program_id(1) == 0)
    def _():
        mod = mod_ref[0]
        u = _normmod(h_ref[...], g_ref[...], mod[row:row + 1], mod[row + 1:row + 2])
        u_ref[...] = u.astype(BF16)

    o_ref[...] = _dot(u_ref[...], w_ref[...]).astype(o_ref.dtype)


def _normmod_swiglu_kernel(h_ref, mod_ref, g_ref, w1_ref, w3_ref, o_ref, u_ref, *, row):
    @pl.when(pl.program_id(1) == 0)
    def _():
        mod = mod_ref[0]
        u = _normmod(h_ref[...], g_ref[...], mod[row:row + 1], mod[row + 1:row + 2])
        u_ref[...] = u.astype(BF16)

    u = u_ref[...]
    o_ref[...] = (_silu(_dot(u, w1_ref[...])) * _dot(u, w3_ref[...])).astype(o_ref.dtype)


def _mod_index(tm, seq, batch):
    return lambda i, j: (jnp.minimum((i * tm) // seq, batch), 0, 0)


def _normmod_mm(h, mod, g, w, *, row, n_rows, tm, seq, batch, tn_pref):
    d, n = w.shape
    tn = _tile(n, tn_pref)
    return pl.pallas_call(
        functools.partial(_normmod_mm_kernel, row=row),
        name="norm_in_proj",
        grid=(n_rows // tm, n // tn),
        in_specs=[pl.BlockSpec((tm, d), lambda i, j: (i, 0)),
                  pl.BlockSpec((1, 8, d), _mod_index(tm, seq, batch)),
                  pl.BlockSpec((1, d), lambda i, j: (0, 0)),
                  pl.BlockSpec((d, tn), lambda i, j: (0, j))],
        out_specs=pl.BlockSpec((tm, tn), lambda i, j: (i, j)),
        out_shape=jax.ShapeDtypeStruct((h.shape[0], n), BF16),
        scratch_shapes=[pltpu.VMEM((tm, d), BF16)],
        compiler_params=_params(("parallel", "arbitrary")),
    )(h, mod, g.reshape(1, d), w)


def _normmod_swiglu(h, mod, g, w1, w3, *, row, n_rows, tm, seq, batch):
    d, n = w1.shape
    tn = _tile(n, 512)
    return pl.pallas_call(
        functools.partial(_normmod_swiglu_kernel, row=row),
        name="norm_swiglu_up",
        grid=(n_rows // tm, n // tn),
        in_specs=[pl.BlockSpec((tm, d), lambda i, j: (i, 0)),
                  pl.BlockSpec((1, 8, d), _mod_index(tm, seq, batch)),
                  pl.BlockSpec((1, d), lambda i, j: (0, 0)),
                  pl.BlockSpec((d, tn), lambda i, j: (0, j)),
                  pl.BlockSpec((d, tn), lambda i, j: (0, j))],
        out_specs=pl.BlockSpec((tm, tn), lambda i, j: (i, j)),
        out_shape=jax.ShapeDtypeStruct((h.shape[0], n), BF16),
        scratch_shapes=[pltpu.VMEM((tm, d), BF16)],
        compiler_params=_params(("parallel", "arbitrary")),
    )(h, mod, g.reshape(1, d), w1, w3)


def _mm_res_kernel(a_ref, w_ref, h_ref, mod_ref, o_ref, *, row):
    gate = mod_ref[0][row:row + 1]
    o_ref[...] = h_ref[...] + gate * _dot(a_ref[...], w_ref[...])


WEIGHT_TILE_BYTES = 8 << 20


def _mm_res(a, w, h, mod, *, row, n_rows, tm, seq, batch):
    kdim, n = w.shape
    tn = _tile(n, WEIGHT_TILE_BYTES // (2 * kdim))
    mod_idx = _mod_index(tm, seq, batch)
    return pl.pallas_call(
        functools.partial(_mm_res_kernel, row=row),
        name="proj_gated_residual",
        grid=(n_rows // tm, n // tn),
        in_specs=[pl.BlockSpec((tm, kdim), lambda i, j: (i, 0)),
                  pl.BlockSpec((kdim, tn), lambda i, j: (0, j)),
                  pl.BlockSpec((tm, tn), lambda i, j: (i, j)),
                  pl.BlockSpec((1, 8, tn), lambda i, j: (mod_idx(i, j)[0], 0, j))],
        out_specs=pl.BlockSpec((tm, tn), lambda i, j: (i, j)),
        out_shape=jax.ShapeDtypeStruct((n_rows, n), F32),
        compiler_params=_params(("parallel", "arbitrary")),
    )(a, w, h, mod)


def _rope_tables(seq):
    t = jnp.arange(seq)
    row = (t // GRID_W).astype(F32)
    col = (t % GRID_W).astype(F32)
    axis_dim = DIFF_QK_DIM // 2
    inv_freq = ROPE_BASE ** (-jnp.arange(0, axis_dim, 2, dtype=F32) / axis_dim)
    ar = row[:, None] * inv_freq
    ac = col[:, None] * inv_freq
    ang = jnp.concatenate([ar, ar, ac, ac], axis=-1)
    reps = LANES // DIFF_QK_DIM
    cos = jnp.tile(jnp.cos(ang), (1, reps))
    sin = jnp.tile(jnp.sin(ang), (1, reps))
    half = DIFF_QK_DIM // 4
    first = (jnp.arange(LANES) % (2 * half)) < half
    sin_dn = jnp.where(first, 0.0, sin)
    sin_up = jnp.where(first, -sin, 0.0)
    return cos, sin_dn, sin_up


def _rope_kernel(q_ref, k_ref, cos_ref, sdn_ref, sup_ref, qo_ref, ko_ref, *, q_scale):
    cos = cos_ref[...]
    sdn = sdn_ref[...]
    sup = sup_ref[...]
    half = DIFF_QK_DIM // 4
    for src, dst, mul in ((q_ref, qo_ref, q_scale), (k_ref, ko_ref, None)):
        for c in range(src.shape[1] // LANES):
            x = src[:, c * LANES:(c + 1) * LANES].astype(F32)
            y = x * cos + pltpu.roll(x, half, 1) * sdn + pltpu.roll(x, LANES - half, 1) * sup
            if mul is not None:
                y = y * mul
            dst[:, c * LANES:(c + 1) * LANES] = y.astype(dst.dtype)


def _rope(p, tabs, *, off, n_lat, seq, tm):
    w = DIFF_HEADS * 2 * DIFF_QK_DIM
    assert off['qd'] % w == 0 and off['kd'] % w == 0
    per = seq // tm
    tab_spec = pl.BlockSpec((tm, LANES), lambda i: (i % per, 0))
    return pl.pallas_call(
        functools.partial(_rope_kernel, q_scale=DIFF_QK_DIM ** -0.5 * math.log2(math.e)),
        name="rope",
        grid=(n_lat // tm,),
        in_specs=[pl.BlockSpec((tm, w), lambda i: (i, off['qd'] // w)),
                  pl.BlockSpec((tm, w), lambda i: (i, off['kd'] // w)),
                  tab_spec, tab_spec, tab_spec],
        out_specs=[pl.BlockSpec((tm, w), lambda i: (i, 0)), pl.BlockSpec((tm, w), lambda i: (i, 0))],
        out_shape=[jax.ShapeDtypeStruct((n_lat, w), BF16), jax.ShapeDtypeStruct((n_lat, w), BF16)],
        compiler_params=_params(("parallel",)),
    )(p, p, *tabs)


def _na_bias_table(rpb, rows):
    del rows
    kh = NA_WIN_H
    cols = jnp.arange(GRID_W)
    c0 = jnp.clip(cols - NA_WIN_W // 2, 0, GRID_W - NA_WIN_W)
    col_in = (cols[None, :] >= c0[:, None]) & (cols[None, :] < c0[:, None] + NA_WIN_W)
    rpb = rpb.astype(F32)
    by_row = jnp.stack([rpb[:, kh - 1 - o:2 * kh - 1 - o, :] for o in range(kh)], axis=1)
    edge = GRID_W - NA_WIN_W
    ext = jnp.concatenate([jnp.repeat(by_row[..., :1], edge, axis=-1), by_row,
                           jnp.repeat(by_row[..., -1:], edge, axis=-1)], axis=-1)
    bias = jnp.stack([ext[..., GRID_W - 1 - q:2 * GRID_W - 1 - q] for q in range(GRID_W)], axis=2)
    bias = jnp.where(col_in[None, None, :, None, :], bias, NEG_INF)
    return bias.reshape(rpb.shape[0], kh, GRID_W, kh * GRID_W)


def _na_kernel(q_ref, k_ref, v_ref, kc_ref, vc_ref, bias_ref, o_ref, *, rb, rows, scale):
    blk = pl.program_id(2)
    kc = kc_ref[...]
    vc = vc_ref[...]
    win = NA_WIN_H * GRID_W
    starts, s_lat, s_ctx = [], [], []
    for j in range(rb):
        r = blk * rb + j
        r0 = jnp.clip(r - NA_WIN_H // 2, 0, rows - NA_WIN_H)
        start = pl.multiple_of(r0 * GRID_W, GRID_W)
        starts.append(start)
        q = q_ref[j * GRID_W:(j + 1) * GRID_W, :]
        s_lat.append(_dot_nt(q, k_ref[pl.ds(start, win), :]) * scale + bias_ref[0, r - r0])
        s_ctx.append(_dot_nt(q, kc) * scale)
    p_lat, p_ctx, dens = [], [], []
    for j in range(rb):
        m = jnp.maximum(jnp.max(s_lat[j], axis=-1, keepdims=True), jnp.max(s_ctx[j], axis=-1, keepdims=True))
        p_l = jnp.exp(s_lat[j] - m)
        p_c = jnp.exp(s_ctx[j] - m)
        dens.append(jnp.sum(p_l, axis=-1, keepdims=True) + jnp.sum(p_c, axis=-1, keepdims=True))
        p_lat.append(p_l.astype(BF16))
        p_ctx.append(p_c.astype(BF16))
    for j in range(rb):
        o = _dot(p_lat[j], v_ref[pl.ds(starts[j], win), :]) + _dot(p_ctx[j], vc)
        o_ref[j * GRID_W:(j + 1) * GRID_W, :] = (o / dens[j]).astype(o_ref.dtype)


def _na_attention(p, bias_tab, *, off, batch, seq, ctx):
    rows = seq // GRID_W
    assert rows >= NA_WIN_H
    rb = 8 if rows % 8 == 0 else 1
    dh = NA_HEAD_DIM
    nblk = rows // rb
    qc, kc_, vc_ = off['qa'] // dh, off['ka'] // dh, off['va'] // dh
    ctx_blk0 = batch * seq // ctx
    win = NA_WIN_H * GRID_W
    return pl.pallas_call(
        functools.partial(_na_kernel, rb=rb, rows=rows, scale=dh ** -0.5),
        name="na_attention",
        grid=(batch, NA_HEADS, nblk),
        in_specs=[pl.BlockSpec((rb * GRID_W, dh), lambda b, h, r: (b * nblk + r, qc + h)),
                  pl.BlockSpec((seq, dh), lambda b, h, r: (b, kc_ + h)),
                  pl.BlockSpec((seq, dh), lambda b, h, r: (b, vc_ + h)),
                  pl.BlockSpec((ctx, dh), lambda b, h, r: (ctx_blk0 + b, kc_ + h)),
                  pl.BlockSpec((ctx, dh), lambda b, h, r: (ctx_blk0 + b, vc_ + h)),
                  pl.BlockSpec((1, NA_WIN_H, GRID_W, win), lambda b, h, r: (h, 0, 0, 0))],
        out_specs=pl.BlockSpec((rb * GRID_W, dh), lambda b, h, r: (b * nblk + r, h)),
        out_shape=jax.ShapeDtypeStruct((batch * seq, NA_HEADS * dh), BF16),
        compiler_params=_params(("parallel", "parallel", "arbitrary")),
    )(p, p, p, p, p, bias_tab)


def _ctx_na_kernel(q_ref, k_ref, v_ref, o_ref, *, scale):
    s = _dot_nt(q_ref[...], k_ref[...]) * scale
    m = jnp.max(s, axis=-1, keepdims=True)
    e = jnp.exp(s - m)
    den = jnp.sum(e, axis=-1, keepdims=True)
    o_ref[...] = (_dot(e.astype(BF16), v_ref[...]) / den).astype(o_ref.dtype)


def _ctx_na_attention(p, *, off, batch, seq, ctx):
    dh = NA_HEAD_DIM
    qc, kc_, vc_ = off['qa'] // dh, off['ka'] // dh, off['va'] // dh
    blk0 = batch * seq // ctx
    return pl.pallas_call(
        functools.partial(_ctx_na_kernel, scale=dh ** -0.5),
        name="ctx_na_attention",
        grid=(batch, NA_HEADS),
        in_specs=[pl.BlockSpec((ctx, dh), lambda b, h: (blk0 + b, qc + h)),
                  pl.BlockSpec((ctx, dh), lambda b, h: (blk0 + b, kc_ + h)),
                  pl.BlockSpec((ctx, dh), lambda b, h: (blk0 + b, vc_ + h))],
        out_specs=pl.BlockSpec((ctx, dh), lambda b, h: (b, h)),
        out_shape=jax.ShapeDtypeStruct((batch * ctx, NA_HEADS * dh), BF16),
        compiler_params=_params(("parallel", "parallel")),
    )(p, p, p)


def _split_maps(q):
    lane = lax.broadcasted_iota(jnp.int32, q.shape, 1)
    zero = jnp.zeros_like(q)
    return jnp.concatenate([jnp.where(lane < DIFF_QK_DIM, q, zero), jnp.where(lane >= DIFF_QK_DIM, q, zero)],
                           axis=0)


def _diff_finish(o1, o2, par_ref):
    lam = par_ref[1:2, :]
    o = o1 - lam * o2
    y = o * lax.rsqrt(jnp.mean(o * o, axis=-1, keepdims=True) + SUBLN_EPS)
    return y * par_ref[0:1, :]


ONES_ROWS = 16


def _diff_kernel(q_ref, k_ref, v_ref, kc_ref, vc_ref, par_ref, o_ref, vt_ref, vct_ref, s_ref, *, tk):
    tq = q_ref.shape[0]
    dv = v_ref.shape[1]
    n_chunks = k_ref.shape[0] // tk

    @pl.when(pl.program_id(2) == 0)
    def _():
        for c in range(n_chunks):
            vt_ref[c, 0:dv, :] = v_ref[c * tk:(c + 1) * tk, :].astype(F32).T.astype(BF16)
            vt_ref[c, dv:, :] = jnp.ones((ONES_ROWS, tk), BF16)
        vct_ref[0:dv, :] = vc_ref[...].astype(F32).T.astype(BF16)
        vct_ref[dv:, :] = jnp.ones((ONES_ROWS, vct_ref.shape[1]), BF16)

    q_t = q_ref[...].astype(F32).T
    row = lax.broadcasted_iota(jnp.int32, q_t.shape, 0)
    qm = jnp.concatenate([jnp.where(row < DIFF_QK_DIM, q_t, 0.0), jnp.where(row >= DIFF_QK_DIM, q_t, 0.0)],
                         axis=1).astype(BF16)

    def scores(c):
        start = pl.multiple_of(c * tk, tk)
        return _dot(k_ref[pl.ds(start, tk), :], qm)

    def consume(s, vtb, carry):
        m, acc = carry
        m_new = jnp.maximum(m, jnp.max(s, axis=0, keepdims=True))
        alpha = jnp.exp2(m - m_new)
        e = jnp.exp2(s - m_new).astype(BF16)
        return m_new, alpha * acc + _dot(vtb, e)

    per_trip = 8 if n_chunks % 8 == 0 else 2

    def body(j, carry):
        c0 = per_trip * j
        for t in range(per_trip):
            s_ref[(t + 1) % 2] = scores(jnp.minimum(c0 + t + 1, n_chunks - 1))
            carry = consume(s_ref[t % 2], vt_ref[c0 + t], carry)
        return carry

    init = (jnp.full((1, 2 * tq), -jnp.inf, F32), jnp.zeros((dv + ONES_ROWS, 2 * tq), F32))
    s_ref[0] = scores(0)
    carry = lax.fori_loop(0, n_chunks // per_trip, body, init)
    m, acc = consume(_dot(kc_ref[...], qm), vct_ref[...], carry)
    o = acc[0:dv] / acc[dv:dv + 1]
    o_ref[...] = _diff_finish(o[:, :tq].T, o[:, tq:].T, par_ref).astype(o_ref.dtype)


def _diff_attention(qr, kr, p, par, *, off, batch, seq, ctx):
    dq2 = 2 * DIFF_QK_DIM
    dv = DIFF_V_DIM
    assert dq2 == LANES and dv == LANES
    tq = _tile(seq, 512)
    tk = _tile(seq, 512)
    assert (seq // tk) % 2 == 0
    nq = seq // tq
    kdc, vdc = off['kd'] // dq2, off['vd'] // dv
    ctx_blk0 = batch * seq // ctx
    return pl.pallas_call(
        functools.partial(_diff_kernel, tk=tk),
        name="diff_attention",
        scratch_shapes=[pltpu.VMEM((seq // tk, dv + ONES_ROWS, tk), BF16), pltpu.VMEM((dv + ONES_ROWS, ctx), BF16),
                        pltpu.VMEM((2, tk, 2 * tq), F32)],
        grid=(batch, DIFF_HEADS, nq),
        in_specs=[pl.BlockSpec((tq, dq2), lambda b, h, i: (b * nq + i, h)),
                  pl.BlockSpec((seq, dq2), lambda b, h, i: (b, h)),
                  pl.BlockSpec((seq, dv), lambda b, h, i: (b, vdc + h)),
                  pl.BlockSpec((ctx, dq2), lambda b, h, i: (ctx_blk0 + b, kdc + h)),
                  pl.BlockSpec((ctx, dv), lambda b, h, i: (ctx_blk0 + b, vdc + h)),
                  pl.BlockSpec((8, dv), lambda b, h, i: (0, 0))],
        out_specs=pl.BlockSpec((tq, dv), lambda b, h, i: (b * nq + i, h)),
        out_shape=jax.ShapeDtypeStruct((batch * seq, DIFF_HEADS * dv), BF16),
        compiler_params=_params(("parallel", "parallel", "arbitrary")),
    )(qr, kr, p, p, p, par)


def _ctx_diff_kernel(q_ref, k_ref, v_ref, par_ref, o_ref, *, scale):
    n = q_ref.shape[0]
    s = _dot_nt(_split_maps(q_ref[...]), k_ref[...]) * scale
    m = jnp.max(s, axis=-1, keepdims=True)
    e = jnp.exp(s - m)
    den = jnp.sum(e, axis=-1, keepdims=True)
    o = _dot(e.astype(BF16), v_ref[...]) / den
    o_ref[...] = _diff_finish(o[:n], o[n:], par_ref).astype(o_ref.dtype)


def _ctx_diff_attention(p, par, *, off, batch, seq, ctx):
    dq2 = 2 * DIFF_QK_DIM
    dv = DIFF_V_DIM
    qdc, kdc, vdc = off['qd'] // dq2, off['kd'] // dq2, off['vd'] // dv
    blk0 = batch * seq // ctx
    return pl.pallas_call(
        functools.partial(_ctx_diff_kernel, scale=DIFF_QK_DIM ** -0.5),
        name="ctx_diff_attention",
        grid=(batch, DIFF_HEADS),
        in_specs=[pl.BlockSpec((ctx, dq2), lambda b, h: (blk0 + b, qdc + h)),
                  pl.BlockSpec((ctx, dq2), lambda b, h: (blk0 + b, kdc + h)),
                  pl.BlockSpec((ctx, dv), lambda b, h: (blk0 + b, vdc + h)),
                  pl.BlockSpec((8, dv), lambda b, h: (0, 0))],
        out_specs=pl.BlockSpec((ctx, dv), lambda b, h: (b, h)),
        out_shape=jax.ShapeDtypeStruct((batch * ctx, DIFF_HEADS * dv), BF16),
        compiler_params=_params(("parallel", "parallel")),
    )(p, p, p, par)


def _ret_tables(log_gamma, reverse):
    cs = RET_CHUNK
    k_scale = RET_QK_DIM ** -0.5
    pos = jnp.arange(cs, dtype=F32)
    lg = log_gamma[:, None]
    rel = pos[:, None] - pos[None, :]
    if reverse:
        rel = -rel
        q_decay = jnp.exp(lg * (cs - pos))
        k_decay = jnp.exp(lg * pos)
    else:
        q_decay = jnp.exp(lg * (pos + 1.0))
        k_decay = jnp.exp(lg * (cs - 1.0 - pos))
    intra = jnp.where(rel >= 0, jnp.exp(lg[:, :, None] * jnp.maximum(rel, 0.0)), 0.0) * k_scale
    chunk_decay = jnp.broadcast_to(jnp.exp(lg * cs)[:, :, None], (RET_HEADS, 1, RET_V_DIM))
    return intra, q_decay[:, :, None], (k_decay * k_scale)[:, :, None], chunk_decay


def _ret_kernel(*refs, nh, final):
    q_refs, k_refs, v_refs = refs[0:nh], refs[nh:2 * nh], refs[2 * nh:3 * nh]
    pos = 3 * nh
    if final:
        g_refs = refs[pos:pos + nh]
        of_ref, gain_ref = refs[pos + nh:pos + nh + 2]
        pos += nh + 2
    intra_ref, qdec_ref, kdec_ref, cd_ref, o_ref, s_ref = refs[pos:pos + 6]
    dv = RET_V_DIM

    @pl.when(pl.program_id(1) == 0)
    def _():
        s_ref[...] = jnp.zeros_like(s_ref)

    atts, outs = [], []
    for h in range(nh):
        atts.append((_dot_nt(q_refs[h][...], k_refs[h][...]) * intra_ref[h]).astype(BF16))
    for h in range(nh):
        qd = (q_refs[h][...].astype(F32) * qdec_ref[h]).astype(BF16)
        outs.append(_dot(atts[h], v_refs[h][...]) + _dot(qd, s_ref[h].astype(BF16)))
    for h in range(nh):
        kd_t = (k_refs[h][...].astype(F32) * kdec_ref[h]).T.astype(BF16)
        s_ref[h] = s_ref[h] * cd_ref[h] + _dot(kd_t, v_refs[h][...])
    for h in range(nh):
        o = outs[h]
        cols = slice(h * dv, (h + 1) * dv)
        if final:
            o = o + of_ref[:, cols]
            mu = jnp.mean(o, axis=-1, keepdims=True)
            var = jnp.mean(jnp.square(o - mu), axis=-1, keepdims=True)
            y = (o - mu) * lax.rsqrt(var + NORM_EPS) * gain_ref[:, cols]
            o_ref[:, cols] = (_silu(g_refs[h][...].astype(F32)) * y).astype(o_ref.dtype)
        else:
            o_ref[:, cols] = o


def _retention(p, tabs_f, tabs_b, gain, *, off, batch, seq, ctx, n_tok):
    cs, nh, dk, dv = RET_CHUNK, RET_HEADS, RET_QK_DIM, RET_V_DIM
    lc, sc = ctx // cs, seq // cs
    ctx_blk0 = batch * seq // cs
    vw = nh * dv
    assert off['qb'] % dk == 0 and off['kb'] % dk == 0 and off['vb'] % dv == 0 and off['gb'] % dv == 0

    def chunk_fwd(b, t):
        return jnp.where(t < lc, ctx_blk0 + b * lc + t, b * sc + (t - lc))

    def chunk_bwd(b, t):
        return jnp.where(t < lc, ctx_blk0 + b * lc + (lc - 1 - t), b * sc + (sc - 1 - (t - lc)))

    def run(chunk, tabs, final, of):
        def head_spec(width, col0, h):
            return pl.BlockSpec((cs, width), lambda b, t: (chunk(b, t), col0 + h))

        in_specs = ([head_spec(dk, off['qb'] // dk, h) for h in range(nh)]
                    + [head_spec(dk, off['kb'] // dk, h) for h in range(nh)]
                    + [head_spec(dv, off['vb'] // dv, h) for h in range(nh)])
        args = [p] * (3 * nh)
        if final:
            in_specs += [head_spec(dv, off['gb'] // dv, h) for h in range(nh)]
            in_specs += [pl.BlockSpec((cs, vw), lambda b, t: (chunk(b, t), 0)),
                         pl.BlockSpec((1, vw), lambda b, t: (0, 0))]
            args += [p] * nh + [of, gain.reshape(1, vw).astype(F32)]
        in_specs += [pl.BlockSpec((nh, cs, cs), lambda b, t: (0, 0, 0)),
                     pl.BlockSpec((nh, cs, 1), lambda b, t: (0, 0, 0)),
                     pl.BlockSpec((nh, cs, 1), lambda b, t: (0, 0, 0)),
                     pl.BlockSpec((nh, 1, dv), lambda b, t: (0, 0, 0))]
        args += list(tabs)
        return pl.pallas_call(
            functools.partial(_ret_kernel, nh=nh, final=final),
            name="retention_bwd_norm_gate" if final else "retention_fwd",
            grid=(batch, lc + sc),
            in_specs=in_specs,
            out_specs=pl.BlockSpec((cs, vw), lambda b, t: (chunk(b, t), 0)),
            out_shape=jax.ShapeDtypeStruct((n_tok, vw), BF16 if final else F32),
            scratch_shapes=[pltpu.VMEM((nh, dk, dv), F32)],
            compiler_params=_params(("parallel", "arbitrary")),
        )(*args)

    o_f = run(chunk_fwd, tabs_f, False, None)
    return run(chunk_bwd, tabs_b, True, o_f)


def _merge_kernel(*refs, lat_tiles):
    oa_ref, ob_ref, od_ref = refs[0:3]
    oa = oa_ref[...]
    od = od_ref[...]
    pos = 3
    if lat_tiles is not None:
        is_ctx = pl.program_id(0) >= lat_tiles
        oa = jnp.where(is_ctx, refs[3][...], oa)
        od = jnp.where(is_ctx, refs[4][...], od)
        pos = 5
    ga_ref, gb_ref, gd_ref, wa_ref, wb_ref, wd_ref, o_ref = refs[pos:pos + 7]
    j = pl.program_id(1)
    y = jax.nn.sigmoid(ga_ref[...].astype(F32)) * _dot(oa, wa_ref[j])
    y += jax.nn.sigmoid(gb_ref[...].astype(F32)) * _dot(ob_ref[...], wb_ref[j])
    y += jax.nn.sigmoid(gd_ref[...].astype(F32)) * _dot(od, wd_ref[j])
    o_ref[...] = y.astype(o_ref.dtype)


def _col_blocked(w, tn):
    k, n = w.shape
    return w.reshape(k, n // tn, tn).transpose(1, 0, 2)


def _merge(oa, ob, od, ctx_pair, p, wa, wb, wd, *, off, n_rows, n_lat, tm):
    d = D_MODEL
    tn = _tile(d, 512)
    assert off['gates'] % tn == 0
    g0 = off['gates'] // tn
    nj = d // tn
    lat_tiles = n_lat // tm

    def lat(a):
        return pl.BlockSpec((tm, a.shape[1]), lambda i, j: (jnp.minimum(i, lat_tiles - 1), 0))

    def ctx_rows(a):
        return pl.BlockSpec((tm, a.shape[1]), lambda i, j: (jnp.maximum(i - lat_tiles, 0), 0))

    def wspec(w):
        return pl.BlockSpec((nj, w.shape[0], tn), lambda i, j: (0, 0, 0))

    def gspec(br):
        return pl.BlockSpec((tm, tn), lambda i, j: (i, g0 + br * nj + j))

    in_specs = [lat(oa), pl.BlockSpec((tm, ob.shape[1]), lambda i, j: (i, 0)), lat(od)]
    args = [oa, ob, od]
    if ctx_pair is not None:
        in_specs += [ctx_rows(ctx_pair[0]), ctx_rows(ctx_pair[1])]
        args += list(ctx_pair)
    in_specs += [gspec(0), gspec(1), gspec(2), wspec(wa), wspec(wb), wspec(wd)]
    args += [p, p, p, _col_blocked(wa, tn), _col_blocked(wb, tn), _col_blocked(wd, tn)]
    return pl.pallas_call(
        functools.partial(_merge_kernel, lat_tiles=lat_tiles if ctx_pair is not None else None),
        name="branch_merge",
        grid=(n_rows // tm, nj),
        in_specs=in_specs,
        out_specs=pl.BlockSpec((tm, tn), lambda i, j: (i, j)),
        out_shape=jax.ShapeDtypeStruct((n_rows, d), BF16),
        compiler_params=_params(("parallel", "arbitrary")),
    )(*args)


def _router_kernel(h_ref, mod_ref, g_ref, rw_ref, rb_ref, u_ref, idx_ref, wt_ref, *, row):
    mod = mod_ref[0]
    u = _normmod(h_ref[...], g_ref[...], mod[row:row + 1], mod[row + 1:row + 2])
    u_hi = u.astype(BF16)
    u_ref[...] = u_hi
    u_lo = (u - u_hi.astype(F32)).astype(BF16)
    w = rw_ref[...]
    w_hi = w.astype(BF16)
    w_lo = (w - w_hi.astype(F32)).astype(BF16)
    logits = _dot(u_hi, w_hi) + _dot(u_hi, w_lo) + _dot(u_lo, w_hi) + rb_ref[...]
    lane = lax.broadcasted_iota(jnp.int32, logits.shape, 1)
    lg = jnp.where(lane < N_EXPERTS, logits, -jnp.inf)
    v1 = jnp.max(lg, axis=-1, keepdims=True)
    lane_f = lane.astype(F32)
    i1 = jnp.min(jnp.where(lg == v1, lane_f, float(LANES)), axis=-1, keepdims=True).astype(jnp.int32)
    lg2 = jnp.where(lane == i1, -jnp.inf, lg)
    v2 = jnp.max(lg2, axis=-1, keepdims=True)
    i2 = jnp.min(jnp.where(lg2 == v2, lane_f, float(LANES)), axis=-1, keepdims=True).astype(jnp.int32)
    e = jnp.exp(v2 - v1)
    w1 = 1.0 / (1.0 + e)
    w2 = e / (1.0 + e)
    idx_ref[...] = jnp.where(lane == 0, i1, jnp.where(lane == 1, i2, 0))
    wt_ref[...] = jnp.where(lane == 0, w1, jnp.where(lane == 1, w2, 0.0))


def _router(h, mod, g, router_w, router_b, *, row, n_rows, tm, seq, batch):
    d = D_MODEL
    rw = jnp.zeros((d, LANES), F32).at[:, :N_EXPERTS].set(router_w.astype(F32))
    rb = jnp.zeros((1, LANES), F32).at[0, :N_EXPERTS].set(router_b.astype(F32))
    mod_idx = _mod_index(tm, seq, batch)
    return pl.pallas_call(
        functools.partial(_router_kernel, row=row),
        name="norm_router",
        grid=(n_rows // tm,),
        in_specs=[pl.BlockSpec((tm, d), lambda i: (i, 0)),
                  pl.BlockSpec((1, 8, d), lambda i: mod_idx(i, 0)),
                  pl.BlockSpec((1, d), lambda i: (0, 0)),
                  pl.BlockSpec((d, LANES), lambda i: (0, 0)),
                  pl.BlockSpec((1, LANES), lambda i: (0, 0))],
        out_specs=[pl.BlockSpec((tm, d), lambda i: (i, 0)),
                   pl.BlockSpec((tm, LANES), lambda i: (i, 0)),
                   pl.BlockSpec((tm, LANES), lambda i: (i, 0))],
        out_shape=[jax.ShapeDtypeStruct((n_rows, d), BF16),
                   jax.ShapeDtypeStruct((n_rows, LANES), jnp.int32),
                   jax.ShapeDtypeStruct((n_rows, LANES), F32)],
        compiler_params=_params(("parallel",)),
    )(h, mod, g.reshape(1, d), rw, rb)


def _route_plan(ridx, rwt, tme):
    n_tok = ridx.shape[0]
    n2 = n_tok * TOP_K
    n_pad = n2 + N_EXPERTS * tme
    n_tiles = n_pad // tme
    e_flat = ridx.reshape(-1)
    order = jnp.argsort(e_flat, stable=True).astype(jnp.int32)
    counts = jnp.sum(e_flat[:, None] == jnp.arange(N_EXPERTS)[None, :], axis=0).astype(jnp.int32)
    padded = ((counts + tme - 1) // tme) * tme
    pend = jnp.cumsum(padded)
    pstart = pend - padded
    ustart = jnp.cumsum(counts) - counts
    tile_start = jnp.arange(n_tiles, dtype=jnp.int32) * tme
    tile_expert = jnp.minimum(jnp.searchsorted(pend, tile_start, side='right'), N_EXPERTS - 1).astype(jnp.int32)
    tile_valid = (tile_start < pend[-1]).astype(jnp.int32)
    slot = jnp.arange(n_pad, dtype=jnp.int32)
    slot_e = jnp.repeat(tile_expert, tme)
    within = slot - pstart[slot_e]
    valid = (within < counts[slot_e]) & (jnp.repeat(tile_valid, tme) > 0)
    src = jnp.clip(ustart[slot_e] + within, 0, n2 - 1)
    assign = order[src]
    perm_tok = jnp.where(valid, assign // TOP_K, 0)
    w_sorted = jnp.where(valid, rwt.reshape(-1)[assign], 0.0)
    e_sorted = e_flat[order]
    pos = jnp.arange(n2, dtype=jnp.int32) - ustart[e_sorted] + pstart[e_sorted]
    slot_of = pos[jnp.argsort(order)].reshape(n_tok, TOP_K)
    return perm_tok, w_sorted, slot_of, tile_expert, tile_valid


def _moe_up_kernel(te_ref, tv_ref, tf_ref, a_ref, w1_ref, w3_ref, o_ref, w1b_ref, w3b_ref):
    del te_ref
    i = pl.program_id(1)

    @pl.when(tf_ref[i] > 0)
    def _():
        w1b_ref[...] = w1_ref[0].astype(BF16)
        w3b_ref[...] = w3_ref[0].astype(BF16)

    @pl.when(tv_ref[i] > 0)
    def _():
        a = a_ref[...]
        o_ref[...] = (_silu(_dot(a, w1b_ref[...])) * _dot(a, w3b_ref[...])).astype(o_ref.dtype)

    @pl.when(tv_ref[i] == 0)
    def _():
        o_ref[...] = jnp.zeros_like(o_ref)


def _moe_up(a, w1, w3, tile_expert, tile_valid, *, tme):
    n_pad, d = a.shape
    f = w1.shape[2]
    tn = _tile(f, 1024)
    tile_first = jnp.concatenate([jnp.ones((1,), jnp.int32),
                                  (tile_expert[1:] != tile_expert[:-1]).astype(jnp.int32)])
    return pl.pallas_call(
        _moe_up_kernel,
        name="moe_up",
        grid_spec=pltpu.PrefetchScalarGridSpec(
            num_scalar_prefetch=3,
            grid=(f // tn, n_pad // tme),
            in_specs=[pl.BlockSpec((tme, d), lambda j, i, te, tv, tf: (i, 0)),
                      pl.BlockSpec((1, d, tn), lambda j, i, te, tv, tf: (te[i], 0, j)),
                      pl.BlockSpec((1, d, tn), lambda j, i, te, tv, tf: (te[i], 0, j))],
            out_specs=pl.BlockSpec((tme, tn), lambda j, i, te, tv, tf: (i, j)),
            scratch_shapes=[pltpu.VMEM((d, tn), BF16), pltpu.VMEM((d, tn), BF16)]),
        out_shape=jax.ShapeDtypeStruct((n_pad, f), BF16),
        compiler_params=_params(("parallel", "arbitrary")),
    )(tile_expert, tile_valid, tile_first, a, w1, w3)


def _moe_down_kernel(te_ref, tv_ref, a_ref, w_ref, ws_ref, o_ref, acc_ref, *, nk):
    i = pl.program_id(0)
    k = pl.program_id(1)

    @pl.when(k == 0)
    def _():
        acc_ref[...] = jnp.zeros_like(acc_ref)

    @pl.when(tv_ref[i] > 0)
    def _():
        acc_ref[...] += _dot(a_ref[...], w_ref[0])

    @pl.when(k == nk - 1)
    def _():
        o_ref[...] = (ws_ref[...] * acc_ref[...]).astype(o_ref.dtype)


def _moe_down(a, w2, w_sorted, tile_expert, tile_valid, *, tme):
    n_pad, f = a.shape
    d = w2.shape[2]
    tk = _tile(f, 1024)
    nk = f // tk
    return pl.pallas_call(
        functools.partial(_moe_down_kernel, nk=nk),
        name="moe_down",
        grid_spec=pltpu.PrefetchScalarGridSpec(
            num_scalar_prefetch=2,
            grid=(n_pad // tme, nk),
            in_specs=[pl.BlockSpec((tme, tk), lambda i, k, te, tv: (i, k)),
                      pl.BlockSpec((1, tk, d), lambda i, k, te, tv: (te[i], k, 0)),
                      pl.BlockSpec((tme, 1), lambda i, k, te, tv: (i, 0))],
            out_specs=pl.BlockSpec((tme, d), lambda i, k, te, tv: (i, 0)),
            scratch_shapes=[pltpu.VMEM((tme, d), F32)]),
        out_shape=jax.ShapeDtypeStruct((n_pad, d), BF16),
        compiler_params=_params(("parallel", "arbitrary")),
    )(tile_expert, tile_valid, a, w2, w_sorted.reshape(n_pad, 1))


def _combine_norm_kernel(h_ref, y1_ref, y2_ref, mod_ref, g_ref, o_ref, *, row):
    gate = mod_ref[0][row:row + 1]
    x = h_ref[...] + gate * (y1_ref[...].astype(F32) + y2_ref[...].astype(F32))
    ms = jnp.mean(x * x, axis=-1, keepdims=True)
    o_ref[...] = x * lax.rsqrt(ms + NORM_EPS) * g_ref[...]


def _combine_norm(h, y12, mod, g, *, row, n_rows, tm, seq, batch):
    d = D_MODEL
    mod_idx = _mod_index(tm, seq, batch)
    rowspec = pl.BlockSpec((tm, d), lambda i: (i, 0))
    nt = n_rows // tm
    return pl.pallas_call(
        functools.partial(_combine_norm_kernel, row=row),
        name="moe_combine_final_norm",
        grid=(nt,),
        in_specs=[rowspec, rowspec, pl.BlockSpec((tm, d), lambda i: (nt + i, 0)),
                  pl.BlockSpec((1, 8, d), lambda i: mod_idx(i, 0)),
                  pl.BlockSpec((1, d), lambda i: (0, 0))],
        out_specs=rowspec,
        out_shape=jax.ShapeDtypeStruct((n_rows, d), F32),
        compiler_params=_params(("parallel",)),
    )(h, y12, y12, mod, g.reshape(1, d))


def _token_mixers(h, mod, lp, layer_idx, rope_tabs, *, dims, need_ctx):
    batch, seq, ctx, n_tok, n_lat, tm = dims
    off, in_w = _layout()
    bf = lambda a: a.astype(BF16)
    p = _normmod_mm(h, mod, lp['norm1'], bf(lp['w_in']), row=0, n_rows=n_tok, tm=tm, seq=seq, batch=batch,
                    tn_pref=768)

    bias_tab = _na_bias_table(lp['na_rpb'], seq // GRID_W)
    oa = _na_attention(p, bias_tab, off=off, batch=batch, seq=seq, ctx=ctx)

    log_gamma = jax.nn.log_sigmoid(lp['ret_decay'].astype(F32))
    ob = _retention(p, _ret_tables(log_gamma[0], False), _ret_tables(log_gamma[1], True), lp['ret_gn'],
                    off=off, batch=batch, seq=seq, ctx=ctx, n_tok=n_tok)

    lam_init = 0.8 - 0.6 * math.exp(-0.3 * layer_idx)
    lq1, lk1, lq2, lk2 = lp['diff_lam'].astype(F32)
    lam = jnp.exp(jnp.sum(lq1 * lk1)) - jnp.exp(jnp.sum(lq2 * lk2)) + lam_init
    par = jnp.zeros((8, DIFF_V_DIM), F32)
    par = par.at[0].set(lp['diff_subln'].astype(F32) * (1.0 - lam_init)).at[1].set(lam)
    qr, kr = _rope(p, rope_tabs, off=off, n_lat=n_lat, seq=seq, tm=tm)
    od = _diff_attention(qr, kr, p, par, off=off, batch=batch, seq=seq, ctx=ctx)

    ctx_pair = None
    if need_ctx:
        ctx_pair = (_ctx_na_attention(p, off=off, batch=batch, seq=seq, ctx=ctx),
                    _ctx_diff_attention(p, par, off=off, batch=batch, seq=seq, ctx=ctx))
    n_rows = n_tok if need_ctx else n_lat
    ymid = _merge(oa, ob, od, ctx_pair, p, bf(lp['w_br_a']), bf(lp['w_br_b']), bf(lp['w_br_c']), off=off,
                  n_rows=n_rows, n_lat=n_lat, tm=tm)
    return _mm_res(ymid, bf(lp['w_out']), h, mod, row=2, n_rows=n_rows, tm=tm, seq=seq, batch=batch)


def _forward(x, c, ctx_tok, c_ctx, layers, final_norm):
    batch, seq, d = x.shape
    ctx = ctx_tok.shape[1]
    n_lat = batch * seq
    n_tok = n_lat + batch * ctx
    tm = min(ROW_TILE, batch * ctx)
    assert d == D_MODEL and seq % tm == 0 and (batch * ctx) % tm == 0 and seq % GRID_W == 0
    assert seq % RET_CHUNK == 0 and ctx % RET_CHUNK == 0 and n_lat % ctx == 0 and batch + 1 <= 8
    dims = (batch, seq, ctx, n_tok, n_lat, tm)
    bf = lambda a: a.astype(BF16)

    h = jnp.concatenate([x.reshape(n_lat, d), ctx_tok.reshape(batch * ctx, d)], axis=0).astype(F32)
    cond8 = jnp.zeros((8, d), F32).at[:batch].set(c).at[batch].set(c_ctx)
    rope_tabs = _rope_tables(seq)
    n_layers = len(layers)
    out = None
    for li, lp in enumerate(layers):
        need_ctx = li < n_layers - 1
        m = _adaln(cond8, lp['w_ada'], lp['b_ada'])
        mod = jnp.zeros((batch + 1, 8, d), F32).at[:, :6].set(m[:batch + 1].reshape(batch + 1, 6, d))
        h = _token_mixers(h, mod, lp, li, rope_tabs, dims=dims, need_ctx=need_ctx)
        n_rows = n_tok if need_ctx else n_lat
        if 'ffn_w1' in lp:
            a = _normmod_swiglu(h, mod, lp['norm2'], bf(lp['ffn_w1']), bf(lp['ffn_w3']), row=3, n_rows=n_rows,
                                tm=tm, seq=seq, batch=batch)
            h = _mm_res(a, bf(lp['ffn_w2']), h, mod, row=5, n_rows=n_rows, tm=tm, seq=seq, batch=batch)
        else:
            assert not need_ctx and li == n_layers - 1
            tme = min(MOE_ROW_TILE, n_lat)
            u, ridx, rwt = _router(h, mod, lp['norm2'], lp['router_w'], lp['router_b'], row=3, n_rows=n_lat,
                                   tm=tm, seq=seq, batch=batch)
            perm_tok, w_sorted, slot_of, tile_expert, tile_valid = _route_plan(ridx[:, :TOP_K], rwt[:, :TOP_K], tme)
            u_sorted = u.at[perm_tok].get(mode='promise_in_bounds')
            a = _moe_up(u_sorted, lp['exp_w1'], lp['exp_w3'], tile_expert, tile_valid, tme=tme)
            y = _moe_down(a, bf(lp['exp_w2']), w_sorted, tile_expert, tile_valid, tme=tme)
            y12 = y.at[slot_of.T.reshape(-1)].get(mode='promise_in_bounds')
            out = _combine_norm(h, y12, mod, final_norm, row=5, n_rows=n_lat, tm=tm, seq=seq, batch=batch)
    return out.reshape(batch, seq, d)


def kernel(x, c, ctx, c_ctx, l0_w_ada, l0_b_ada, l0_norm1, l0_w_in, l0_na_rpb, l0_ret_decay, l0_ret_gn, l0_diff_lam, l0_diff_subln, l0_w_br_a, l0_w_br_b, l0_w_br_c, l0_w_out, l0_norm2, l0_ffn_w1, l0_ffn_w3, l0_ffn_w2, l1_w_ada, l1_b_ada, l1_norm1, l1_w_in, l1_na_rpb, l1_ret_decay, l1_ret_gn, l1_diff_lam, l1_diff_subln, l1_w_br_a, l1_w_br_b, l1_w_br_c, l1_w_out, l1_norm2, l1_router_w, l1_router_b, l1_exp_w1, l1_exp_w3, l1_exp_w2, final_norm):
    layers = (
        dict(w_ada=l0_w_ada, b_ada=l0_b_ada, norm1=l0_norm1, w_in=l0_w_in, na_rpb=l0_na_rpb,
             ret_decay=l0_ret_decay, ret_gn=l0_ret_gn, diff_lam=l0_diff_lam, diff_subln=l0_diff_subln,
             w_br_a=l0_w_br_a, w_br_b=l0_w_br_b, w_br_c=l0_w_br_c, w_out=l0_w_out, norm2=l0_norm2,
             ffn_w1=l0_ffn_w1, ffn_w3=l0_ffn_w3, ffn_w2=l0_ffn_w2),
        dict(w_ada=l1_w_ada, b_ada=l1_b_ada, norm1=l1_norm1, w_in=l1_w_in, na_rpb=l1_na_rpb,
             ret_decay=l1_ret_decay, ret_gn=l1_ret_gn, diff_lam=l1_diff_lam, diff_subln=l1_diff_subln,
             w_br_a=l1_w_br_a, w_br_b=l1_w_br_b, w_br_c=l1_w_br_c, w_out=l1_w_out, norm2=l1_norm2,
             router_w=l1_router_w, router_b=l1_router_b, exp_w1=l1_exp_w1, exp_w3=l1_exp_w3,
             exp_w2=l1_exp_w2),
    )
    return _forward(x, c, ctx, c_ctx, layers, final_norm)
```

```python
import functools
import math

import jax
import jax.numpy as jnp
from jax import lax
from jax.experimental import pallas as pl
from jax.experimental.pallas import tpu as pltpu

D_MODEL = 2048
GRID_W = 64
NA_HEADS = 6
NA_HEAD_DIM = 128
NA_WIN_H = 8
NA_WIN_W = 16
RET_HEADS = 4
RET_QK_DIM = 128
RET_V_DIM = 256
RET_CHUNK = 128
DIFF_HEADS = 6
DIFF_QK_DIM = 64
DIFF_V_DIM = 128
ROPE_BASE = 10000.0
N_EXPERTS = 8
TOP_K = 2
N_BRANCHES = 3
NORM_EPS = 1e-6
SUBLN_EPS = 1e-5
NEG_INF = -1e30

LANES = 128
ROW_TILE = 512
WIDE_ROW_FACTOR = 2
MOE_ROW_TILE = 512
VMEM_LIMIT = 56 << 20

F32 = jnp.float32
BF16 = jnp.bfloat16


def _params(sem, vmem=VMEM_LIMIT):
    return pltpu.CompilerParams(dimension_semantics=sem, vmem_limit_bytes=vmem)


def _dot(a, b):
    return jnp.dot(a, b, preferred_element_type=F32)


def _dot_nt(a, b):
    return lax.dot_general(a, b, (((1,), (1,)), ((), ())), preferred_element_type=F32)


def _silu(x):
    return x * jax.nn.sigmoid(x)


def _tile(n, pref):
    if n <= pref:
        return n
    t = (pref // LANES) * LANES
    while t >= LANES:
        if n % t == 0:
            return t
        t -= LANES
    raise ValueError("no lane-aligned tile for %d" % n)


def _layout():
    na_w = NA_HEADS * NA_HEAD_DIM
    ret_qk_w = RET_HEADS * RET_QK_DIM
    ret_v_w = RET_HEADS * RET_V_DIM
    diff_qk_w = DIFF_HEADS * 2 * DIFF_QK_DIM
    diff_v_w = DIFF_HEADS * DIFF_V_DIM
    names = ('qa', 'ka', 'va', 'qb', 'kb', 'vb', 'gb', 'qd', 'kd', 'vd', 'gates')
    widths = (na_w, na_w, na_w, ret_qk_w, ret_qk_w, ret_v_w, ret_v_w, diff_qk_w, diff_qk_w, diff_v_w,
              N_BRANCHES * D_MODEL)
    off = {}
    acc = 0
    for n, w in zip(names, widths):
        off[n] = acc
        acc += w
    return off, acc


def _normmod(x, g, shift, scale):
    ms = jnp.mean(x * x, axis=-1, keepdims=True)
    y = x * lax.rsqrt(ms + NORM_EPS) * g
    return y * (1.0 + scale) + shift


def _ada_kernel(c_ref, w_ref, b_ref, o_ref):
    a = _silu(c_ref[...]).astype(BF16)
    o_ref[...] = _dot(a, w_ref[...].astype(BF16)) + b_ref[...]


def _adaln(cond8, w_ada, b_ada):
    d, n = w_ada.shape
    tn = _tile(n, 1024)
    return pl.pallas_call(
        _ada_kernel,
        name="adaln",
        grid=(n // tn,),
        in_specs=[pl.BlockSpec((8, d), lambda j: (0, 0)),
                  pl.BlockSpec((d, tn), lambda j: (0, j)),
                  pl.BlockSpec((1, tn), lambda j: (0, j))],
        out_specs=pl.BlockSpec((8, tn), lambda j: (0, j)),
        out_shape=jax.ShapeDtypeStruct((8, n), F32),
        compiler_params=_params(("parallel",)),
    )(cond8, w_ada, b_ada.reshape(1, n))


def _normmod_mm_kernel(h_ref, mod_ref, g_ref, w_ref, o_ref, u_ref, *, row):
    @pl.when(pl.program_id(1) == 0)
    def _():
        mod = mod_ref[0]
        u = _normmod(h_ref[...], g_ref[...], mod[row:row + 1], mod[row + 1:row + 2])
        u_ref[...] = u.astype(BF16)

    o_ref[...] = _dot(u_ref[...], w_ref[...]).astype(o_ref.dtype)


def _normmod_swiglu_kernel(h_ref, mod_ref, g_ref, w1_ref, w3_ref, o_ref, u_ref, *, row):
    @pl.when(pl.program_id(1) == 0)
    def _():
        mod = mod_ref[0]
        u = _normmod(h_ref[...], g_ref[...], mod[row:row + 1], mod[row + 1:row + 2])
        u_ref[...] = u.astype(BF16)

    u = u_ref[...]
    o_ref[...] = (_silu(_dot(u, w1_ref[...])) * _dot(u, w3_ref[...])).astype(o_ref.dtype)


def _mod_index(tm, seq, batch):
    return lambda i, j: (jnp.minimum((i * tm) // seq, batch), 0, 0)


def _normmod_mm(h, mod, g, w, *, row, n_rows, tm, seq, batch, tn_pref):
    d, n = w.shape
    tn = _tile(n, tn_pref)
    return pl.pallas_call(
        functools.partial(_normmod_mm_kernel, row=row),
        name="norm_in_proj",
        grid=(pl.cdiv(n_rows, tm), n // tn),
        in_specs=[pl.BlockSpec((tm, d), lambda i, j: (i, 0)),
                  pl.BlockSpec((1, 8, d), _mod_index(tm, seq, batch)),
                  pl.BlockSpec((1, d), lambda i, j: (0, 0)),
                  pl.BlockSpec((d, tn), lambda i, j: (0, j))],
        out_specs=pl.BlockSpec((tm, tn), lambda i, j: (i, j)),
        out_shape=jax.ShapeDtypeStruct((h.shape[0], n), BF16),
        scratch_shapes=[pltpu.VMEM((tm, d), BF16)],
        compiler_params=_params(("parallel", "arbitrary")),
    )(h, mod, g.reshape(1, d), w)


def _normmod_swiglu(h, mod, g, w1, w3, *, row, n_rows, tm, seq, batch):
    d, n = w1.shape
    tn = _tile(n, 512)
    return pl.pallas_call(
        functools.partial(_normmod_swiglu_kernel, row=row),
        name="norm_swiglu_up",
        grid=(pl.cdiv(n_rows, tm), n // tn),
        in_specs=[pl.BlockSpec((tm, d), lambda i, j: (i, 0)),
                  pl.BlockSpec((1, 8, d), _mod_index(tm, seq, batch)),
                  pl.BlockSpec((1, d), lambda i, j: (0, 0)),
                  pl.BlockSpec((d, tn), lambda i, j: (0, j)),
                  pl.BlockSpec((d, tn), lambda i, j: (0, j))],
        out_specs=pl.BlockSpec((tm, tn), lambda i, j: (i, j)),
        out_shape=jax.ShapeDtypeStruct((h.shape[0], n), BF16),
        scratch_shapes=[pltpu.VMEM((tm, d), BF16)],
        compiler_params=_params(("parallel", "arbitrary")),
    )(h, mod, g.reshape(1, d), w1, w3)


def _mm_res_kernel(a_ref, w_ref, h_ref, mod_ref, o_ref, *, row):
    gate = mod_ref[0][row:row + 1]
    o_ref[...] = h_ref[...] + gate * _dot(a_ref[...], w_ref[...])


def _mm_res(a, w, h, mod, *, row, n_rows, tm, seq, batch, weight_tile_bytes):
    kdim, n = w.shape
    tn = _tile(n, weight_tile_bytes // (2 * kdim))
    mod_idx = _mod_index(tm, seq, batch)
    return pl.pallas_call(
        functools.partial(_mm_res_kernel, row=row),
        name="proj_gated_residual",
        grid=(pl.cdiv(n_rows, tm), n // tn),
        in_specs=[pl.BlockSpec((tm, kdim), lambda i, j: (i, 0)),
                  pl.BlockSpec((kdim, tn), lambda i, j: (0, j)),
                  pl.BlockSpec((tm, tn), lambda i, j: (i, j)),
                  pl.BlockSpec((1, 8, tn), lambda i, j: (mod_idx(i, j)[0], 0, j))],
        out_specs=pl.BlockSpec((tm, tn), lambda i, j: (i, j)),
        out_shape=jax.ShapeDtypeStruct((n_rows, n), F32),
        compiler_params=_params(("parallel", "arbitrary")),
    )(a, w, h, mod)


def _rope_tables(seq):
    t = jnp.arange(seq)
    row = (t // GRID_W).astype(F32)
    col = (t % GRID_W).astype(F32)
    axis_dim = DIFF_QK_DIM // 2
    inv_freq = ROPE_BASE ** (-jnp.arange(0, axis_dim, 2, dtype=F32) / axis_dim)
    ar = row[:, None] * inv_freq
    ac = col[:, None] * inv_freq
    ang = jnp.concatenate([ar, ar, ac, ac], axis=-1)
    reps = LANES // DIFF_QK_DIM
    cos = jnp.tile(jnp.cos(ang), (1, reps))
    sin = jnp.tile(jnp.sin(ang), (1, reps))
    half = DIFF_QK_DIM // 4
    first = (jnp.arange(LANES) % (2 * half)) < half
    sin_dn = jnp.where(first, 0.0, sin)
    sin_up = jnp.where(first, -sin, 0.0)
    return cos, sin_dn, sin_up


def _rope_kernel(q_ref, k_ref, cos_ref, sdn_ref, sup_ref, qo_ref, ko_ref, *, q_scale):
    cos = cos_ref[...]
    sdn = sdn_ref[...]
    sup = sup_ref[...]
    half = DIFF_QK_DIM // 4
    for src, dst, mul in ((q_ref, qo_ref, q_scale), (k_ref, ko_ref, None)):
        for c in range(src.shape[1] // LANES):
            x = src[:, c * LANES:(c + 1) * LANES].astype(F32)
            y = x * cos + pltpu.roll(x, half, 1) * sdn + pltpu.roll(x, LANES - half, 1) * sup
            if mul is not None:
                y = y * mul
            dst[:, c * LANES:(c + 1) * LANES] = y.astype(dst.dtype)


def _rope(p, tabs, *, off, n_lat, seq, tm):
    w = DIFF_HEADS * 2 * DIFF_QK_DIM
    assert off['qd'] % w == 0 and off['kd'] % w == 0
    per = seq // tm
    tab_spec = pl.BlockSpec((tm, LANES), lambda i: (i % per, 0))
    return pl.pallas_call(
        functools.partial(_rope_kernel, q_scale=DIFF_QK_DIM ** -0.5 * math.log2(math.e)),
        name="rope",
        grid=(n_lat // tm,),
        in_specs=[pl.BlockSpec((tm, w), lambda i: (i, off['qd'] // w)),
                  pl.BlockSpec((tm, w), lambda i: (i, off['kd'] // w)),
                  tab_spec, tab_spec, tab_spec],
        out_specs=[pl.BlockSpec((tm, w), lambda i: (i, 0)), pl.BlockSpec((tm, w), lambda i: (i, 0))],
        out_shape=[jax.ShapeDtypeStruct((n_lat, w), BF16), jax.ShapeDtypeStruct((n_lat, w), BF16)],
        compiler_params=_params(("parallel",)),
    )(p, p, *tabs)


def _na_bias_table(rpb, rows):
    del rows
    kh = NA_WIN_H
    cols = jnp.arange(GRID_W)
    c0 = jnp.clip(cols - NA_WIN_W // 2, 0, GRID_W - NA_WIN_W)
    col_in = (cols[None, :] >= c0[:, None]) & (cols[None, :] < c0[:, None] + NA_WIN_W)
    rpb = rpb.astype(F32)
    by_row = jnp.stack([rpb[:, kh - 1 - o:2 * kh - 1 - o, :] for o in range(kh)], axis=1)
    edge = GRID_W - NA_WIN_W
    ext = jnp.concatenate([jnp.repeat(by_row[..., :1], edge, axis=-1), by_row,
                           jnp.repeat(by_row[..., -1:], edge, axis=-1)], axis=-1)
    bias = jnp.stack([ext[..., GRID_W - 1 - q:2 * GRID_W - 1 - q] for q in range(GRID_W)], axis=2)
    bias = jnp.where(col_in[None, None, :, None, :], bias, NEG_INF)
    return bias.reshape(rpb.shape[0], kh, GRID_W, kh * GRID_W)


def _na_kernel(q_ref, k_ref, v_ref, kc_ref, vc_ref, bias_ref, o_ref, *, rb, rows, scale):
    blk = pl.program_id(2)
    kc = kc_ref[...]
    vc = vc_ref[...]
    win = NA_WIN_H * GRID_W
    starts, s_lat, s_ctx = [], [], []
    for j in range(rb):
        r = blk * rb + j
        r0 = jnp.clip(r - NA_WIN_H // 2, 0, rows - NA_WIN_H)
        start = pl.multiple_of(r0 * GRID_W, GRID_W)
        starts.append(start)
        q = q_ref[j * GRID_W:(j + 1) * GRID_W, :]
        s_lat.append(_dot_nt(q, k_ref[pl.ds(start, win), :]) * scale + bias_ref[0, r - r0])
        s_ctx.append(_dot_nt(q, kc) * scale)
    p_lat, p_ctx, dens = [], [], []
    for j in range(rb):
        m = jnp.maximum(jnp.max(s_lat[j], axis=-1, keepdims=True), jnp.max(s_ctx[j], axis=-1, keepdims=True))
        p_l = jnp.exp(s_lat[j] - m)
        p_c = jnp.exp(s_ctx[j] - m)
        dens.append(jnp.sum(p_l, axis=-1, keepdims=True) + jnp.sum(p_c, axis=-1, keepdims=True))
        p_lat.append(p_l.astype(BF16))
        p_ctx.append(p_c.astype(BF16))
    for j in range(rb):
        o = _dot(p_lat[j], v_ref[pl.ds(starts[j], win), :]) + _dot(p_ctx[j], vc)
        o_ref[j * GRID_W:(j + 1) * GRID_W, :] = (o / dens[j]).astype(o_ref.dtype)


def _na_attention(p, bias_tab, *, off, batch, seq, ctx):
    rows = seq // GRID_W
    assert rows >= NA_WIN_H
    rb = 8 if rows % 8 == 0 else 1
    dh = NA_HEAD_DIM
    nblk = rows // rb
    qc, kc_, vc_ = off['qa'] // dh, off['ka'] // dh, off['va'] // dh
    ctx_blk0 = batch * seq // ctx
    win = NA_WIN_H * GRID_W
    return pl.pallas_call(
        functools.partial(_na_kernel, rb=rb, rows=rows, scale=dh ** -0.5),
        name="na_attention",
        grid=(batch, NA_HEADS, nblk),
        in_specs=[pl.BlockSpec((rb * GRID_W, dh), lambda b, h, r: (b * nblk + r, qc + h)),
                  pl.BlockSpec((seq, dh), lambda b, h, r: (b, kc_ + h)),
                  pl.BlockSpec((seq, dh), lambda b, h, r: (b, vc_ + h)),
                  pl.BlockSpec((ctx, dh), lambda b, h, r: (ctx_blk0 + b, kc_ + h)),
                  pl.BlockSpec((ctx, dh), lambda b, h, r: (ctx_blk0 + b, vc_ + h)),
                  pl.BlockSpec((1, NA_WIN_H, GRID_W, win), lambda b, h, r: (h, 0, 0, 0))],
        out_specs=pl.BlockSpec((rb * GRID_W, dh), lambda b, h, r: (b * nblk + r, h)),
        out_shape=jax.ShapeDtypeStruct((batch * seq, NA_HEADS * dh), BF16),
        compiler_params=_params(("parallel", "parallel", "arbitrary")),
    )(p, p, p, p, p, bias_tab)


def _ctx_na_kernel(q_ref, k_ref, v_ref, o_ref, *, scale):
    s = _dot_nt(q_ref[...], k_ref[...]) * scale
    m = jnp.max(s, axis=-1, keepdims=True)
    e = jnp.exp(s - m)
    den = jnp.sum(e, axis=-1, keepdims=True)
    o_ref[...] = (_dot(e.astype(BF16), v_ref[...]) / den).astype(o_ref.dtype)


def _ctx_na_attention(p, *, off, batch, seq, ctx):
    dh = NA_HEAD_DIM
    qc, kc_, vc_ = off['qa'] // dh, off['ka'] // dh, off['va'] // dh
    blk0 = batch * seq // ctx
    return pl.pallas_call(
        functools.partial(_ctx_na_kernel, scale=dh ** -0.5),
        name="ctx_na_attention",
        grid=(batch, NA_HEADS),
        in_specs=[pl.BlockSpec((ctx, dh), lambda b, h: (blk0 + b, qc + h)),
                  pl.BlockSpec((ctx, dh), lambda b, h: (blk0 + b, kc_ + h)),
                  pl.BlockSpec((ctx, dh), lambda b, h: (blk0 + b, vc_ + h))],
        out_specs=pl.BlockSpec((ctx, dh), lambda b, h: (b, h)),
        out_shape=jax.ShapeDtypeStruct((batch * ctx, NA_HEADS * dh), BF16),
        compiler_params=_params(("parallel", "parallel")),
    )(p, p, p)


def _split_maps(q):
    lane = lax.broadcasted_iota(jnp.int32, q.shape, 1)
    zero = jnp.zeros_like(q)
    return jnp.concatenate([jnp.where(lane < DIFF_QK_DIM, q, zero), jnp.where(lane >= DIFF_QK_DIM, q, zero)],
                           axis=0)


def _diff_finish(o1, o2, par_ref):
    lam = par_ref[1:2, :]
    o = o1 - lam * o2
    y = o * lax.rsqrt(jnp.mean(o * o, axis=-1, keepdims=True) + SUBLN_EPS)
    return y * par_ref[0:1, :]


ONES_ROWS = 16


def _diff_kernel(q_ref, k_ref, v_ref, kc_ref, vc_ref, par_ref, o_ref, vt_ref, vct_ref, s_ref, *, tk):
    tq = q_ref.shape[0]
    dv = v_ref.shape[1]
    n_chunks = k_ref.shape[0] // tk

    @pl.when(pl.program_id(2) == 0)
    def _():
        for c in range(n_chunks):
            vt_ref[c, 0:dv, :] = v_ref[c * tk:(c + 1) * tk, :].astype(F32).T.astype(BF16)
            vt_ref[c, dv:, :] = jnp.ones((ONES_ROWS, tk), BF16)
        vct_ref[0:dv, :] = vc_ref[...].astype(F32).T.astype(BF16)
        vct_ref[dv:, :] = jnp.ones((ONES_ROWS, vct_ref.shape[1]), BF16)

    q_t = q_ref[...].astype(F32).T
    row = lax.broadcasted_iota(jnp.int32, q_t.shape, 0)
    qm = jnp.concatenate([jnp.where(row < DIFF_QK_DIM, q_t, 0.0), jnp.where(row >= DIFF_QK_DIM, q_t, 0.0)],
                         axis=1).astype(BF16)

    def scores(c):
        start = pl.multiple_of(c * tk, tk)
        return _dot(k_ref[pl.ds(start, tk), :], qm)

    def consume(s, vtb, carry):
        m, acc = carry
        m_new = jnp.maximum(m, jnp.max(s, axis=0, keepdims=True))
        alpha = jnp.exp2(m - m_new)
        e = jnp.exp2(s - m_new).astype(BF16)
        return m_new, alpha * acc + _dot(vtb, e)

    per_trip = 8 if n_chunks % 8 == 0 else 2

    def body(j, carry):
        c0 = per_trip * j
        for t in range(per_trip):
            s_ref[(t + 1) % 2] = scores(jnp.minimum(c0 + t + 1, n_chunks - 1))
            carry = consume(s_ref[t % 2], vt_ref[c0 + t], carry)
        return carry

    init = (jnp.full((1, 2 * tq), -jnp.inf, F32), jnp.zeros((dv + ONES_ROWS, 2 * tq), F32))
    s_ref[0] = scores(0)
    carry = lax.fori_loop(0, n_chunks // per_trip, body, init)
    m, acc = consume(_dot(kc_ref[...], qm), vct_ref[...], carry)
    o = acc[0:dv] / acc[dv:dv + 1]
    o_ref[...] = _diff_finish(o[:, :tq].T, o[:, tq:].T, par_ref).astype(o_ref.dtype)


def _diff_attention(qr, kr, p, par, *, off, batch, seq, ctx):
    dq2 = 2 * DIFF_QK_DIM
    dv = DIFF_V_DIM
    assert dq2 == LANES and dv == LANES
    tq = _tile(seq, 512)
    tk = _tile(seq, 512)
    assert (seq // tk) % 2 == 0
    nq = seq // tq
    kdc, vdc = off['kd'] // dq2, off['vd'] // dv
    ctx_blk0 = batch * seq // ctx
    return pl.pallas_call(
        functools.partial(_diff_kernel, tk=tk),
        name="diff_attention",
        scratch_shapes=[pltpu.VMEM((seq // tk, dv + ONES_ROWS, tk), BF16), pltpu.VMEM((dv + ONES_ROWS, ctx), BF16),
                        pltpu.VMEM((2, tk, 2 * tq), F32)],
        grid=(batch, DIFF_HEADS, nq),
        in_specs=[pl.BlockSpec((tq, dq2), lambda b, h, i: (b * nq + i, h)),
                  pl.BlockSpec((seq, dq2), lambda b, h, i: (b, h)),
                  pl.BlockSpec((seq, dv), lambda b, h, i: (b, vdc + h)),
                  pl.BlockSpec((ctx, dq2), lambda b, h, i: (ctx_blk0 + b, kdc + h)),
                  pl.BlockSpec((ctx, dv), lambda b, h, i: (ctx_blk0 + b, vdc + h)),
                  pl.BlockSpec((8, dv), lambda b, h, i: (0, 0))],
        out_specs=pl.BlockSpec((tq, dv), lambda b, h, i: (b * nq + i, h)),
        out_shape=jax.ShapeDtypeStruct((batch * seq, DIFF_HEADS * dv), BF16),
        compiler_params=_params(("parallel", "parallel", "arbitrary")),
    )(qr, kr, p, p, p, par)


def _ctx_diff_kernel(q_ref, k_ref, v_ref, par_ref, o_ref, *, scale):
    n = q_ref.shape[0]
    s = _dot_nt(_split_maps(q_ref[...]), k_ref[...]) * scale
    m = jnp.max(s, axis=-1, keepdims=True)
    e = jnp.exp(s - m)
    den = jnp.sum(e, axis=-1, keepdims=True)
    o = _dot(e.astype(BF16), v_ref[...]) / den
    o_ref[...] = _diff_finish(o[:n], o[n:], par_ref).astype(o_ref.dtype)


def _ctx_diff_attention(p, par, *, off, batch, seq, ctx):
    dq2 = 2 * DIFF_QK_DIM
    dv = DIFF_V_DIM
    qdc, kdc, vdc = off['qd'] // dq2, off['kd'] // dq2, off['vd'] // dv
    blk0 = batch * seq // ctx
    return pl.pallas_call(
        functools.partial(_ctx_diff_kernel, scale=DIFF_QK_DIM ** -0.5),
        name="ctx_diff_attention",
        grid=(batch, DIFF_HEADS),
        in_specs=[pl.BlockSpec((ctx, dq2), lambda b, h: (blk0 + b, qdc + h)),
                  pl.BlockSpec((ctx, dq2), lambda b, h: (blk0 + b, kdc + h)),
                  pl.BlockSpec((ctx, dv), lambda b, h: (blk0 + b, vdc + h)),
                  pl.BlockSpec((8, dv), lambda b, h: (0, 0))],
        out_specs=pl.BlockSpec((ctx, dv), lambda b, h: (b, h)),
        out_shape=jax.ShapeDtypeStruct((batch * ctx, DIFF_HEADS * dv), BF16),
        compiler_params=_params(("parallel", "parallel")),
    )(p, p, p, par)


def _ret_tables(log_gamma, reverse):
    cs = RET_CHUNK
    k_scale = RET_QK_DIM ** -0.5
    pos = jnp.arange(cs, dtype=F32)
    lg = log_gamma[:, None]
    rel = pos[:, None] - pos[None, :]
    if reverse:
        rel = -rel
        q_decay = jnp.exp(lg * (cs - pos))
        k_decay = jnp.exp(lg * pos)
    else:
        q_decay = jnp.exp(lg * (pos + 1.0))
        k_decay = jnp.exp(lg * (cs - 1.0 - pos))
    intra = jnp.where(rel >= 0, jnp.exp(lg[:, :, None] * jnp.maximum(rel, 0.0)), 0.0) * k_scale
    chunk_decay = jnp.broadcast_to(jnp.exp(lg * cs)[:, :, None], (RET_HEADS, 1, RET_V_DIM))
    return intra, q_decay[:, :, None], (k_decay * k_scale)[:, :, None], chunk_decay


def _ret_kernel(*refs, nh, final, sub_order):
    q_refs, k_refs, v_refs = refs[0:nh], refs[nh:2 * nh], refs[2 * nh:3 * nh]
    pos = 3 * nh
    if final:
        g_refs = refs[pos:pos + nh]
        of_ref, gain_ref = refs[pos + nh:pos + nh + 2]
        pos += nh + 2
    intra_ref, qdec_ref, kdec_ref, cd_ref, o_ref, s_ref = refs[pos:pos + 6]
    dv = RET_V_DIM

    @pl.when(pl.program_id(1) == 0)
    def _():
        s_ref[...] = jnp.zeros_like(s_ref)

    cs = RET_CHUNK
    for sub in sub_order:
        rows = slice(sub * cs, (sub + 1) * cs)
        atts, outs = [], []
        for h in range(nh):
            atts.append((_dot_nt(q_refs[h][rows, :], k_refs[h][rows, :]) * intra_ref[h]).astype(BF16))
        for h in range(nh):
            qd = (q_refs[h][rows, :].astype(F32) * qdec_ref[h]).astype(BF16)
            outs.append(_dot(atts[h], v_refs[h][rows, :]) + _dot(qd, s_ref[h].astype(BF16)))
        for h in range(nh):
            kd_t = (k_refs[h][rows, :].astype(F32) * kdec_ref[h]).T.astype(BF16)
            s_ref[h] = s_ref[h] * cd_ref[h] + _dot(kd_t, v_refs[h][rows, :])
        for h in range(nh):
            o = outs[h]
            cols = slice(h * dv, (h + 1) * dv)
            if final:
                o = o + of_ref[rows, cols]
                mu = jnp.mean(o, axis=-1, keepdims=True)
                var = jnp.mean(jnp.square(o - mu), axis=-1, keepdims=True)
                y = (o - mu) * lax.rsqrt(var + NORM_EPS) * gain_ref[:, cols]
                o_ref[rows, cols] = (_silu(g_refs[h][rows, :].astype(F32)) * y).astype(o_ref.dtype)
            else:
                o_ref[rows, cols] = o


def _retention(p, tabs_f, tabs_b, gain, *, off, batch, seq, ctx, n_tok):
    cs, nh, dk, dv = RET_CHUNK, RET_HEADS, RET_QK_DIM, RET_V_DIM
    per = 2 if (ctx // cs) % 2 == 0 and (seq // cs) % 2 == 0 else 1
    blk = per * cs
    lc, sc = ctx // blk, seq // blk
    ctx_blk0 = batch * seq // blk
    vw = nh * dv
    assert off['qb'] % dk == 0 and off['kb'] % dk == 0 and off['vb'] % dv == 0 and off['gb'] % dv == 0

    def chunk_fwd(b, t):
        return jnp.where(t < lc, ctx_blk0 + b * lc + t, b * sc + (t - lc))

    def chunk_bwd(b, t):
        return jnp.where(t < lc, ctx_blk0 + b * lc + (lc - 1 - t), b * sc + (sc - 1 - (t - lc)))

    def run(chunk, tabs, final, of):
        def head_spec(width, col0, h):
            return pl.BlockSpec((blk, width), lambda b, t: (chunk(b, t), col0 + h))

        in_specs = ([head_spec(dk, off['qb'] // dk, h) for h in range(nh)]
                    + [head_spec(dk, off['kb'] // dk, h) for h in range(nh)]
                    + [head_spec(dv, off['vb'] // dv, h) for h in range(nh)])
        args = [p] * (3 * nh)
        if final:
            in_specs += [head_spec(dv, off['gb'] // dv, h) for h in range(nh)]
            in_specs += [pl.BlockSpec((blk, vw), lambda b, t: (chunk(b, t), 0)),
                         pl.BlockSpec((1, vw), lambda b, t: (0, 0))]
            args += [p] * nh + [of, gain.reshape(1, vw).astype(F32)]
        in_specs += [pl.BlockSpec((nh, cs, cs), lambda b, t: (0, 0, 0)),
                     pl.BlockSpec((nh, cs, 1), lambda b, t: (0, 0, 0)),
                     pl.BlockSpec((nh, cs, 1), lambda b, t: (0, 0, 0)),
                     pl.BlockSpec((nh, 1, dv), lambda b, t: (0, 0, 0))]
        args += list(tabs)
        return pl.pallas_call(
            functools.partial(_ret_kernel, nh=nh, final=final,
                              sub_order=tuple(reversed(range(per))) if final else tuple(range(per))),
            name="retention_bwd_norm_gate" if final else "retention_fwd",
            grid=(batch, lc + sc),
            in_specs=in_specs,
            out_specs=pl.BlockSpec((blk, vw), lambda b, t: (chunk(b, t), 0)),
            out_shape=jax.ShapeDtypeStruct((n_tok, vw), BF16 if final else F32),
            scratch_shapes=[pltpu.VMEM((nh, dk, dv), F32)],
            compiler_params=_params(("parallel", "arbitrary")),
        )(*args)

    o_f = run(chunk_fwd, tabs_f, False, None)
    return run(chunk_bwd, tabs_b, True, o_f)


def _merge_kernel(*refs, lat_tiles):
    oa_ref, ob_ref, od_ref = refs[0:3]
    oa = oa_ref[...]
    od = od_ref[...]
    pos = 3
    if lat_tiles is not None:
        is_ctx = pl.program_id(0) >= lat_tiles
        oa = jnp.where(is_ctx, refs[3][...], oa)
        od = jnp.where(is_ctx, refs[4][...], od)
        pos = 5
    ga_ref, gb_ref, gd_ref, wa_ref, wb_ref, wd_ref, o_ref = refs[pos:pos + 7]
    j = pl.program_id(1)
    y = jax.nn.sigmoid(ga_ref[...].astype(F32)) * _dot(oa, wa_ref[j])
    y += jax.nn.sigmoid(gb_ref[...].astype(F32)) * _dot(ob_ref[...], wb_ref[j])
    y += jax.nn.sigmoid(gd_ref[...].astype(F32)) * _dot(od, wd_ref[j])
    o_ref[...] = y.astype(o_ref.dtype)


def _col_blocked(w, tn):
    k, n = w.shape
    return w.reshape(k, n // tn, tn).transpose(1, 0, 2)


def _merge(oa, ob, od, ctx_pair, p, wa, wb, wd, *, off, n_rows, n_lat, tm):
    d = D_MODEL
    tn = _tile(d, 512)
    assert off['gates'] % tn == 0
    g0 = off['gates'] // tn
    nj = d // tn
    lat_tiles = n_lat // tm

    def lat(a):
        return pl.BlockSpec((tm, a.shape[1]), lambda i, j: (jnp.minimum(i, lat_tiles - 1), 0))

    def ctx_rows(a):
        return pl.BlockSpec((tm, a.shape[1]), lambda i, j: (jnp.maximum(i - lat_tiles, 0), 0))

    def wspec(w):
        return pl.BlockSpec((nj, w.shape[0], tn), lambda i, j: (0, 0, 0))

    def gspec(br):
        return pl.BlockSpec((tm, tn), lambda i, j: (i, g0 + br * nj + j))

    in_specs = [lat(oa), pl.BlockSpec((tm, ob.shape[1]), lambda i, j: (i, 0)), lat(od)]
    args = [oa, ob, od]
    if ctx_pair is not None:
        in_specs += [ctx_rows(ctx_pair[0]), ctx_rows(ctx_pair[1])]
        args += list(ctx_pair)
    in_specs += [gspec(0), gspec(1), gspec(2), wspec(wa), wspec(wb), wspec(wd)]
    args += [p, p, p, _col_blocked(wa, tn), _col_blocked(wb, tn), _col_blocked(wd, tn)]
    return pl.pallas_call(
        functools.partial(_merge_kernel, lat_tiles=lat_tiles if ctx_pair is not None else None),
        name="branch_merge",
        grid=(n_rows // tm, nj),
        in_specs=in_specs,
        out_specs=pl.BlockSpec((tm, tn), lambda i, j: (i, j)),
        out_shape=jax.ShapeDtypeStruct((n_rows, d), BF16),
        compiler_params=_params(("parallel", "arbitrary")),
    )(*args)


def _router_kernel(h_ref, mod_ref, g_ref, rw_ref, rb_ref, u_ref, idx_ref, wt_ref, *, row):
    mod = mod_ref[0]
    u = _normmod(h_ref[...], g_ref[...], mod[row:row + 1], mod[row + 1:row + 2])
    u_hi = u.astype(BF16)
    u_ref[...] = u_hi
    u_lo = (u - u_hi.astype(F32)).astype(BF16)
    w = rw_ref[...]
    w_hi = w.astype(BF16)
    w_lo = (w - w_hi.astype(F32)).astype(BF16)
    logits = _dot(u_hi, w_hi) + _dot(u_hi, w_lo) + _dot(u_lo, w_hi) + rb_ref[...]
    lane = lax.broadcasted_iota(jnp.int32, logits.shape, 1)
    lg = jnp.where(lane < N_EXPERTS, logits, -jnp.inf)
    v1 = jnp.max(lg, axis=-1, keepdims=True)
    lane_f = lane.astype(F32)
    i1 = jnp.min(jnp.where(lg == v1, lane_f, float(LANES)), axis=-1, keepdims=True).astype(jnp.int32)
    lg2 = jnp.where(lane == i1, -jnp.inf, lg)
    v2 = jnp.max(lg2, axis=-1, keepdims=True)
    i2 = jnp.min(jnp.where(lg2 == v2, lane_f, float(LANES)), axis=-1, keepdims=True).astype(jnp.int32)
    e = jnp.exp(v2 - v1)
    w1 = 1.0 / (1.0 + e)
    w2 = e / (1.0 + e)
    idx_ref[...] = jnp.where(lane == 0, i1, jnp.where(lane == 1, i2, 0))
    wt_ref[...] = jnp.where(lane == 0, w1, jnp.where(lane == 1, w2, 0.0))


def _router(h, mod, g, router_w, router_b, *, row, n_rows, tm, seq, batch):
    d = D_MODEL
    rw = jnp.zeros((d, LANES), F32).at[:, :N_EXPERTS].set(router_w.astype(F32))
    rb = jnp.zeros((1, LANES), F32).at[0, :N_EXPERTS].set(router_b.astype(F32))
    mod_idx = _mod_index(tm, seq, batch)
    return pl.pallas_call(
        functools.partial(_router_kernel, row=row),
        name="norm_router",
        grid=(n_rows // tm,),
        in_specs=[pl.BlockSpec((tm, d), lambda i: (i, 0)),
                  pl.BlockSpec((1, 8, d), lambda i: mod_idx(i, 0)),
                  pl.BlockSpec((1, d), lambda i: (0, 0)),
                  pl.BlockSpec((d, LANES), lambda i: (0, 0)),
                  pl.BlockSpec((1, LANES), lambda i: (0, 0))],
        out_specs=[pl.BlockSpec((tm, d), lambda i: (i, 0)),
                   pl.BlockSpec((tm, LANES), lambda i: (i, 0)),
                   pl.BlockSpec((tm, LANES), lambda i: (i, 0))],
        out_shape=[jax.ShapeDtypeStruct((n_rows, d), BF16),
                   jax.ShapeDtypeStruct((n_rows, LANES), jnp.int32),
                   jax.ShapeDtypeStruct((n_rows, LANES), F32)],
        compiler_params=_params(("parallel",)),
    )(h, mod, g.reshape(1, d), rw, rb)


def _route_plan(ridx, rwt, tme):
    n_tok = ridx.shape[0]
    n2 = n_tok * TOP_K
    n_pad = n2 + N_EXPERTS * tme
    n_tiles = n_pad // tme
    e_flat = ridx.reshape(-1)
    order = jnp.argsort(e_flat, stable=True).astype(jnp.int32)
    counts = jnp.sum(e_flat[:, None] == jnp.arange(N_EXPERTS)[None, :], axis=0).astype(jnp.int32)
    padded = ((counts + tme - 1) // tme) * tme
    pend = jnp.cumsum(padded)
    pstart = pend - padded
    ustart = jnp.cumsum(counts) - counts
    tile_start = jnp.arange(n_tiles, dtype=jnp.int32) * tme
    tile_expert = jnp.minimum(jnp.searchsorted(pend, tile_start, side='right'), N_EXPERTS - 1).astype(jnp.int32)
    tile_valid = (tile_start < pend[-1]).astype(jnp.int32)
    slot = jnp.arange(n_pad, dtype=jnp.int32)
    slot_e = jnp.repeat(tile_expert, tme)
    within = slot - pstart[slot_e]
    valid = (within < counts[slot_e]) & (jnp.repeat(tile_valid, tme) > 0)
    src = jnp.clip(ustart[slot_e] + within, 0, n2 - 1)
    assign = order[src]
    perm_tok = jnp.where(valid, assign // TOP_K, 0)
    w_sorted = jnp.where(valid, rwt.reshape(-1)[assign], 0.0)
    e_sorted = e_flat[order]
    pos = jnp.arange(n2, dtype=jnp.int32) - ustart[e_sorted] + pstart[e_sorted]
    slot_of = pos[jnp.argsort(order)].reshape(n_tok, TOP_K)
    return perm_tok, w_sorted, slot_of, tile_expert, tile_valid


def _moe_up_kernel(te_ref, tv_ref, tf_ref, a_ref, w1_ref, w3_ref, o_ref, w1b_ref, w3b_ref):
    del te_ref
    i = pl.program_id(1)

    @pl.when(tf_ref[i] > 0)
    def _():
        w1b_ref[...] = w1_ref[0].astype(BF16)
        w3b_ref[...] = w3_ref[0].astype(BF16)

    @pl.when(tv_ref[i] > 0)
    def _():
        a = a_ref[...]
        o_ref[...] = (_silu(_dot(a, w1b_ref[...])) * _dot(a, w3b_ref[...])).astype(o_ref.dtype)

    @pl.when(tv_ref[i] == 0)
    def _():
        o_ref[...] = jnp.zeros_like(o_ref)


def _moe_up(a, w1, w3, tile_expert, tile_valid, *, tme):
    n_pad, d = a.shape
    f = w1.shape[2]
    tn = _tile(f, 1024)
    tile_first = jnp.concatenate([jnp.ones((1,), jnp.int32),
                                  (tile_expert[1:] != tile_expert[:-1]).astype(jnp.int32)])
    return pl.pallas_call(
        _moe_up_kernel,
        name="moe_up",
        grid_spec=pltpu.PrefetchScalarGridSpec(
            num_scalar_prefetch=3,
            grid=(f // tn, n_pad // tme),
            in_specs=[pl.BlockSpec((tme, d), lambda j, i, te, tv, tf: (i, 0)),
                      pl.BlockSpec((1, d, tn), lambda j, i, te, tv, tf: (te[i], 0, j)),
                      pl.BlockSpec((1, d, tn), lambda j, i, te, tv, tf: (te[i], 0, j))],
            out_specs=pl.BlockSpec((tme, tn), lambda j, i, te, tv, tf: (i, j)),
            scratch_shapes=[pltpu.VMEM((d, tn), BF16), pltpu.VMEM((d, tn), BF16)]),
        out_shape=jax.ShapeDtypeStruct((n_pad, f), BF16),
        compiler_params=_params(("parallel", "arbitrary")),
    )(tile_expert, tile_valid, tile_first, a, w1, w3)


def _moe_down_kernel(te_ref, tv_ref, a_ref, w_ref, ws_ref, o_ref):
    del te_ref
    i = pl.program_id(1)

    @pl.when(tv_ref[i] > 0)
    def _():
        o_ref[...] = (ws_ref[...] * _dot(a_ref[...], w_ref[0])).astype(o_ref.dtype)

    @pl.when(tv_ref[i] == 0)
    def _():
        o_ref[...] = jnp.zeros_like(o_ref)


def _moe_down(a, w2, w_sorted, tile_expert, tile_valid, *, tme):
    n_pad, f = a.shape
    d = w2.shape[2]
    tn = _tile(d, 1024)
    return pl.pallas_call(
        _moe_down_kernel,
        name="moe_down",
        grid_spec=pltpu.PrefetchScalarGridSpec(
            num_scalar_prefetch=2,
            grid=(d // tn, n_pad // tme),
            in_specs=[pl.BlockSpec((tme, f), lambda j, i, te, tv: (i, 0)),
                      pl.BlockSpec((1, f, tn), lambda j, i, te, tv: (te[i], 0, j)),
                      pl.BlockSpec((tme, 1), lambda j, i, te, tv: (i, 0))],
            out_specs=pl.BlockSpec((tme, tn), lambda j, i, te, tv: (i, j))),
        out_shape=jax.ShapeDtypeStruct((n_pad, d), BF16),
        compiler_params=_params(("parallel", "arbitrary")),
    )(tile_expert, tile_valid, a, w2, w_sorted.reshape(n_pad, 1))


def _combine_norm_kernel(h_ref, y1_ref, y2_ref, mod_ref, g_ref, o_ref, *, row):
    gate = mod_ref[0][row:row + 1]
    x = h_ref[...] + gate * (y1_ref[...].astype(F32) + y2_ref[...].astype(F32))
    ms = jnp.mean(x * x, axis=-1, keepdims=True)
    o_ref[...] = x * lax.rsqrt(ms + NORM_EPS) * g_ref[...]


def _combine_norm(h, y12, mod, g, *, row, n_rows, tm, seq, batch):
    d = D_MODEL
    mod_idx = _mod_index(tm, seq, batch)
    rowspec = pl.BlockSpec((tm, d), lambda i: (i, 0))
    nt = n_rows // tm
    return pl.pallas_call(
        functools.partial(_combine_norm_kernel, row=row),
        name="moe_combine_final_norm",
        grid=(nt,),
        in_specs=[rowspec, rowspec, pl.BlockSpec((tm, d), lambda i: (nt + i, 0)),
                  pl.BlockSpec((1, 8, d), lambda i: mod_idx(i, 0)),
                  pl.BlockSpec((1, d), lambda i: (0, 0))],
        out_specs=rowspec,
        out_shape=jax.ShapeDtypeStruct((n_rows, d), F32),
        compiler_params=_params(("parallel",)),
    )(h, y12, y12, mod, g.reshape(1, d))


def _token_mixers(h, mod, lp, layer_idx, rope_tabs, *, dims, need_ctx):
    batch, seq, ctx, n_tok, n_lat, tm = dims
    off, in_w = _layout()
    bf = lambda a: a.astype(BF16)
    p = _normmod_mm(h, mod, lp['norm1'], bf(lp['w_in']), row=0, n_rows=n_tok, tm=WIDE_ROW_FACTOR * tm, seq=seq,
                    batch=batch, tn_pref=768)

    bias_tab = _na_bias_table(lp['na_rpb'], seq // GRID_W)
    oa = _na_attention(p, bias_tab, off=off, batch=batch, seq=seq, ctx=ctx)

    log_gamma = jax.nn.log_sigmoid(lp['ret_decay'].astype(F32))
    ob = _retention(p, _ret_tables(log_gamma[0], False), _ret_tables(log_gamma[1], True), lp['ret_gn'],
                    off=off, batch=batch, seq=seq, ctx=ctx, n_tok=n_tok)

    lam_init = 0.8 - 0.6 * math.exp(-0.3 * layer_idx)
    lq1, lk1, lq2, lk2 = lp['diff_lam'].astype(F32)
    lam = jnp.exp(jnp.sum(lq1 * lk1)) - jnp.exp(jnp.sum(lq2 * lk2)) + lam_init
    par = jnp.zeros((8, DIFF_V_DIM), F32)
    par = par.at[0].set(lp['diff_subln'].astype(F32) * (1.0 - lam_init)).at[1].set(lam)
    qr, kr = _rope(p, rope_tabs, off=off, n_lat=n_lat, seq=seq, tm=tm)
    od = _diff_attention(qr, kr, p, par, off=off, batch=batch, seq=seq, ctx=ctx)

    ctx_pair = None
    if need_ctx:
        ctx_pair = (_ctx_na_attention(p, off=off, batch=batch, seq=seq, ctx=ctx),
                    _ctx_diff_attention(p, par, off=off, batch=batch, seq=seq, ctx=ctx))
    n_rows = n_tok if need_ctx else n_lat
    ymid = _merge(oa, ob, od, ctx_pair, p, bf(lp['w_br_a']), bf(lp['w_br_b']), bf(lp['w_br_c']), off=off,
                  n_rows=n_rows, n_lat=n_lat, tm=tm)
    return _mm_res(ymid, bf(lp['w_out']), h, mod, row=2, n_rows=n_rows, tm=tm, seq=seq, batch=batch,
                   weight_tile_bytes=8 << 20)


def _forward(x, c, ctx_tok, c_ctx, layers, final_norm):
    batch, seq, d = x.shape
    ctx = ctx_tok.shape[1]
    n_lat = batch * seq
    n_tok = n_lat + batch * ctx
    tm = min(ROW_TILE, batch * ctx)
    assert d == D_MODEL and seq % tm == 0 and (batch * ctx) % tm == 0 and seq % GRID_W == 0
    assert seq % RET_CHUNK == 0 and ctx % RET_CHUNK == 0 and n_lat % ctx == 0 and batch + 1 <= 8
    dims = (batch, seq, ctx, n_tok, n_lat, tm)
    bf = lambda a: a.astype(BF16)

    h = jnp.concatenate([x.reshape(n_lat, d), ctx_tok.reshape(batch * ctx, d)], axis=0).astype(F32)
    cond8 = jnp.zeros((8, d), F32).at[:batch].set(c).at[batch].set(c_ctx)
    rope_tabs = _rope_tables(seq)
    n_layers = len(layers)
    out = None
    for li, lp in enumerate(layers):
        need_ctx = li < n_layers - 1
        m = _adaln(cond8, lp['w_ada'], lp['b_ada'])
        mod = jnp.zeros((batch + 1, 8, d), F32).at[:, :6].set(m[:batch + 1].reshape(batch + 1, 6, d))
        h = _token_mixers(h, mod, lp, li, rope_tabs, dims=dims, need_ctx=need_ctx)
        n_rows = n_tok if need_ctx else n_lat
        if 'ffn_w1' in lp:
            a = _normmod_swiglu(h, mod, lp['norm2'], bf(lp['ffn_w1']), bf(lp['ffn_w3']), row=3, n_rows=n_rows,
                                tm=WIDE_ROW_FACTOR * tm, seq=seq, batch=batch)
            h = _mm_res(a, bf(lp['ffn_w2']), h, mod, row=5, n_rows=n_rows, tm=WIDE_ROW_FACTOR * tm, seq=seq,
                        batch=batch, weight_tile_bytes=3 << 20)
        else:
            assert not need_ctx and li == n_layers - 1
            tme = min(MOE_ROW_TILE, n_lat)
            u, ridx, rwt = _router(h, mod, lp['norm2'], lp['router_w'], lp['router_b'], row=3, n_rows=n_lat,
                                   tm=tm, seq=seq, batch=batch)
            perm_tok, w_sorted, slot_of, tile_expert, tile_valid = _route_plan(ridx[:, :TOP_K], rwt[:, :TOP_K], tme)
            u_sorted = u.at[perm_tok].get(mode='promise_in_bounds')
            a = _moe_up(u_sorted, lp['exp_w1'], lp['exp_w3'], tile_expert, tile_valid, tme=tme)
            y = _moe_down(a, bf(lp['exp_w2']), w_sorted, tile_expert, tile_valid, tme=tme)
            y12 = y.at[slot_of.T.reshape(-1)].get(mode='promise_in_bounds')
            out = _combine_norm(h, y12, mod, final_norm, row=5, n_rows=n_lat, tm=tm, seq=seq, batch=batch)
    return out.reshape(batch, seq, d)


def kernel(x, c, ctx, c_ctx, l0_w_ada, l0_b_ada, l0_norm1, l0_w_in, l0_na_rpb, l0_ret_decay, l0_ret_gn, l0_diff_lam, l0_diff_subln, l0_w_br_a, l0_w_br_b, l0_w_br_c, l0_w_out, l0_norm2, l0_ffn_w1, l0_ffn_w3, l0_ffn_w2, l1_w_ada, l1_b_ada, l1_norm1, l1_w_in, l1_na_rpb, l1_ret_decay, l1_ret_gn, l1_diff_lam, l1_diff_subln, l1_w_br_a, l1_w_br_b, l1_w_br_c, l1_w_out, l1_norm2, l1_router_w, l1_router_b, l1_exp_w1, l1_exp_w3, l1_exp_w2, final_norm):
    layers = (
        dict(w_ada=l0_w_ada, b_ada=l0_b_ada, norm1=l0_norm1, w_in=l0_w_in, na_rpb=l0_na_rpb,
             ret_decay=l0_ret_decay, ret_gn=l0_ret_gn, diff_lam=l0_diff_lam, diff_subln=l0_diff_subln,
             w_br_a=l0_w_br_a, w_br_b=l0_w_br_b, w_br_c=l0_w_br_c, w_out=l0_w_out, norm2=l0_norm2,
             ffn_w1=l0_ffn_w1, ffn_w3=l0_ffn_w3, ffn_w2=l0_ffn_w2),
        dict(w_ada=l1_w_ada, b_ada=l1_b_ada, norm1=l1_norm1, w_in=l1_w_in, na_rpb=l1_na_rpb,
             ret_decay=l1_ret_decay, ret_gn=l1_ret_gn, diff_lam=l1_diff_lam, diff_subln=l1_diff_subln,
             w_br_a=l1_w_br_a, w_br_b=l1_w_br_b, w_br_c=l1_w_br_c, w_out=l1_w_out, norm2=l1_norm2,
             router_w=l1_router_w, router_b=l1_router_b, exp_w1=l1_exp_w1, exp_w3=l1_exp_w3,
             exp_w2=l1_exp_w2),
    )
    return _forward(x, c, ctx, c_ctx, layers, final_norm)
```

```python
import functools
import math

import jax
import jax.numpy as jnp
from jax import lax
from jax.experimental import pallas as pl
from jax.experimental.pallas import tpu as pltpu

D_MODEL = 2048
GRID_W = 64
NA_HEADS = 6
NA_HEAD_DIM = 128
NA_WIN_H = 8
NA_WIN_W = 16
RET_HEADS = 4
RET_QK_DIM = 128
RET_V_DIM = 256
RET_CHUNK = 128
DIFF_HEADS = 6
DIFF_QK_DIM = 64
DIFF_V_DIM = 128
ROPE_BASE = 10000.0
N_EXPERTS = 8
TOP_K = 2
N_BRANCHES = 3
NORM_EPS = 1e-6
SUBLN_EPS = 1e-5
NEG_INF = -1e30

LANES = 128
ROW_TILE = 512
WIDE_ROW_FACTOR = 2
MOE_ROW_TILE = 512
VMEM_LIMIT = 56 << 20

F32 = jnp.float32
BF16 = jnp.bfloat16


def _params(sem, vmem=VMEM_LIMIT):
    return pltpu.CompilerParams(dimension_semantics=sem, vmem_limit_bytes=vmem)


def _dot(a, b):
    return jnp.dot(a, b, preferred_element_type=F32)


def _dot_nt(a, b):
    return lax.dot_general(a, b, (((1,), (1,)), ((), ())), preferred_element_type=F32)


def _silu(x):
    return x * jax.nn.sigmoid(x)


def _tile(n, pref):
    if n <= pref:
        return n
    t = (pref // LANES) * LANES
    while t >= LANES:
        if n % t == 0:
            return t
        t -= LANES
    raise ValueError("no lane-aligned tile for %d" % n)


def _layout():
    na_w = NA_HEADS * NA_HEAD_DIM
    ret_qk_w = RET_HEADS * RET_QK_DIM
    ret_v_w = RET_HEADS * RET_V_DIM
    diff_qk_w = DIFF_HEADS * 2 * DIFF_QK_DIM
    diff_v_w = DIFF_HEADS * DIFF_V_DIM
    names = ('qa', 'ka', 'va', 'qb', 'kb', 'vb', 'gb', 'qd', 'kd', 'vd', 'gates')
    widths = (na_w, na_w, na_w, ret_qk_w, ret_qk_w, ret_v_w, ret_v_w, diff_qk_w, diff_qk_w, diff_v_w,
              N_BRANCHES * D_MODEL)
    off = {}
    acc = 0
    for n, w in zip(names, widths):
        off[n] = acc
        acc += w
    return off, acc


def _normmod(x, g, shift, scale):
    ms = jnp.mean(x * x, axis=-1, keepdims=True)
    y = x * lax.rsqrt(ms + NORM_EPS) * g
    return y * (1.0 + scale) + shift


def _ada_kernel(c_ref, w_ref, b_ref, o_ref):
    a = _silu(c_ref[...]).astype(BF16)
    o_ref[...] = _dot(a, w_ref[...].astype(BF16)) + b_ref[...]


def _adaln(cond8, w_ada, b_ada):
    d, n = w_ada.shape
    tn = _tile(n, 1024)
    return pl.pallas_call(
        _ada_kernel,
        name="adaln",
        grid=(n // tn,),
        in_specs=[pl.BlockSpec((8, d), lambda j: (0, 0)),
                  pl.BlockSpec((d, tn), lambda j: (0, j)),
                  pl.BlockSpec((1, tn), lambda j: (0, j))],
        out_specs=pl.BlockSpec((8, tn), lambda j: (0, j)),
        out_shape=jax.ShapeDtypeStruct((8, n), F32),
        compiler_params=_params(("parallel",)),
    )(cond8, w_ada, b_ada.reshape(1, n))


def _normmod_mm_kernel(h_ref, mod_ref, g_ref, w_ref, o_ref, u_ref, *, row):
    @pl.when(pl.program_id(1) == 0)
    def _():
        mod = mod_ref[0]
        u = _normmod(h_ref[...], g_ref[...], mod[row:row + 1], mod[row + 1:row + 2])
        u_ref[...] = u.astype(BF16)

    o_ref[...] = _dot(u_ref[...], w_ref[...]).astype(o_ref.dtype)


def _normmod_swiglu_kernel(h_ref, mod_ref, g_ref, w1_ref, w3_ref, o_ref, u_ref, *, row):
    @pl.when(pl.program_id(1) == 0)
    def _():
        mod = mod_ref[0]
        u = _normmod(h_ref[...], g_ref[...], mod[row:row + 1], mod[row + 1:row + 2])
        u_ref[...] = u.astype(BF16)

    u = u_ref[...]
    o_ref[...] = (_silu(_dot(u, w1_ref[...])) * _dot(u, w3_ref[...])).astype(o_ref.dtype)


def _mod_index(tm, seq, batch):
    return lambda i, j: (jnp.minimum((i * tm) // seq, batch), 0, 0)


def _normmod_mm(h, mod, g, w, *, row, n_rows, tm, seq, batch, tn_pref):
    d, n = w.shape
    tn = _tile(n, tn_pref)
    return pl.pallas_call(
        functools.partial(_normmod_mm_kernel, row=row),
        name="norm_in_proj",
        grid=(pl.cdiv(n_rows, tm), n // tn),
        in_specs=[pl.BlockSpec((tm, d), lambda i, j: (i, 0)),
                  pl.BlockSpec((1, 8, d), _mod_index(tm, seq, batch)),
                  pl.BlockSpec((1, d), lambda i, j: (0, 0)),
                  pl.BlockSpec((d, tn), lambda i, j: (0, j))],
        out_specs=pl.BlockSpec((tm, tn), lambda i, j: (i, j)),
        out_shape=jax.ShapeDtypeStruct((h.shape[0], n), BF16),
        scratch_shapes=[pltpu.VMEM((tm, d), BF16)],
        compiler_params=_params(("parallel", "arbitrary")),
    )(h, mod, g.reshape(1, d), w)


def _normmod_swiglu(h, mod, g, w1, w3, *, row, n_rows, tm, seq, batch):
    d, n = w1.shape
    tn = _tile(n, 512)
    return pl.pallas_call(
        functools.partial(_normmod_swiglu_kernel, row=row),
        name="norm_swiglu_up",
        grid=(pl.cdiv(n_rows, tm), n // tn),
        in_specs=[pl.BlockSpec((tm, d), lambda i, j: (i, 0)),
                  pl.BlockSpec((1, 8, d), _mod_index(tm, seq, batch)),
                  pl.BlockSpec((1, d), lambda i, j: (0, 0)),
                  pl.BlockSpec((d, tn), lambda i, j: (0, j)),
                  pl.BlockSpec((d, tn), lambda i, j: (0, j))],
        out_specs=pl.BlockSpec((tm, tn), lambda i, j: (i, j)),
        out_shape=jax.ShapeDtypeStruct((h.shape[0], n), BF16),
        scratch_shapes=[pltpu.VMEM((tm, d), BF16)],
        compiler_params=_params(("parallel", "arbitrary")),
    )(h, mod, g.reshape(1, d), w1, w3)


def _mm_res_kernel(a_ref, w_ref, h_ref, mod_ref, o_ref, *, row):
    gate = mod_ref[0][row:row + 1]
    o_ref[...] = h_ref[...] + gate * _dot(a_ref[...], w_ref[...])


def _mm_res(a, w, h, mod, *, row, n_rows, tm, seq, batch, weight_tile_bytes):
    kdim, n = w.shape
    tn = _tile(n, weight_tile_bytes // (2 * kdim))
    mod_idx = _mod_index(tm, seq, batch)
    return pl.pallas_call(
        functools.partial(_mm_res_kernel, row=row),
        name="proj_gated_residual",
        grid=(pl.cdiv(n_rows, tm), n // tn),
        in_specs=[pl.BlockSpec((tm, kdim), lambda i, j: (i, 0)),
                  pl.BlockSpec((kdim, tn), lambda i, j: (0, j)),
                  pl.BlockSpec((tm, tn), lambda i, j: (i, j)),
                  pl.BlockSpec((1, 8, tn), lambda i, j: (mod_idx(i, j)[0], 0, j))],
        out_specs=pl.BlockSpec((tm, tn), lambda i, j: (i, j)),
        out_shape=jax.ShapeDtypeStruct((n_rows, n), F32),
        compiler_params=_params(("parallel", "arbitrary")),
    )(a, w, h, mod)


def _rope_tables(seq):
    t = jnp.arange(seq)
    row = (t // GRID_W).astype(F32)
    col = (t % GRID_W).astype(F32)
    axis_dim = DIFF_QK_DIM // 2
    inv_freq = ROPE_BASE ** (-jnp.arange(0, axis_dim, 2, dtype=F32) / axis_dim)
    ar = row[:, None] * inv_freq
    ac = col[:, None] * inv_freq
    ang = jnp.concatenate([ar, ar, ac, ac], axis=-1)
    reps = LANES // DIFF_QK_DIM
    cos = jnp.tile(jnp.cos(ang), (1, reps))
    sin = jnp.tile(jnp.sin(ang), (1, reps))
    half = DIFF_QK_DIM // 4
    first = (jnp.arange(LANES) % (2 * half)) < half
    sin_dn = jnp.where(first, 0.0, sin)
    sin_up = jnp.where(first, -sin, 0.0)
    return cos, sin_dn, sin_up


def _rope_kernel(q_ref, k_ref, cos_ref, sdn_ref, sup_ref, qo_ref, ko_ref, *, q_scale):
    cos = cos_ref[...]
    sdn = sdn_ref[...]
    sup = sup_ref[...]
    half = DIFF_QK_DIM // 4
    for src, dst, mul in ((q_ref, qo_ref, q_scale), (k_ref, ko_ref, None)):
        for c in range(src.shape[1] // LANES):
            x = src[:, c * LANES:(c + 1) * LANES].astype(F32)
            y = x * cos + pltpu.roll(x, half, 1) * sdn + pltpu.roll(x, LANES - half, 1) * sup
            if mul is not None:
                y = y * mul
            dst[:, c * LANES:(c + 1) * LANES] = y.astype(dst.dtype)


def _rope(p, tabs, *, off, n_lat, seq, tm):
    w = DIFF_HEADS * 2 * DIFF_QK_DIM
    assert off['qd'] % w == 0 and off['kd'] % w == 0
    per = seq // tm
    tab_spec = pl.BlockSpec((tm, LANES), lambda i: (i % per, 0))
    return pl.pallas_call(
        functools.partial(_rope_kernel, q_scale=DIFF_QK_DIM ** -0.5 * math.log2(math.e)),
        name="rope",
        grid=(n_lat // tm,),
        in_specs=[pl.BlockSpec((tm, w), lambda i: (i, off['qd'] // w)),
                  pl.BlockSpec((tm, w), lambda i: (i, off['kd'] // w)),
                  tab_spec, tab_spec, tab_spec],
        out_specs=[pl.BlockSpec((tm, w), lambda i: (i, 0)), pl.BlockSpec((tm, w), lambda i: (i, 0))],
        out_shape=[jax.ShapeDtypeStruct((n_lat, w), BF16), jax.ShapeDtypeStruct((n_lat, w), BF16)],
        compiler_params=_params(("parallel",)),
    )(p, p, *tabs)


def _na_bias_table(rpb, rows):
    del rows
    kh = NA_WIN_H
    cols = jnp.arange(GRID_W)
    c0 = jnp.clip(cols - NA_WIN_W // 2, 0, GRID_W - NA_WIN_W)
    col_in = (cols[None, :] >= c0[:, None]) & (cols[None, :] < c0[:, None] + NA_WIN_W)
    rpb = rpb.astype(F32)
    by_row = jnp.stack([rpb[:, kh - 1 - o:2 * kh - 1 - o, :] for o in range(kh)], axis=1)
    edge = GRID_W - NA_WIN_W
    period = 2 * GRID_W
    ext = jnp.concatenate([jnp.repeat(by_row[..., :1], edge, axis=-1), by_row,
                           jnp.repeat(by_row[..., -1:], edge + 1, axis=-1)], axis=-1)
    z = jnp.tile(ext, (1, 1, 1, GRID_W))[..., :GRID_W * (period - 1)]
    z = z.reshape(ext.shape[:3] + (GRID_W, period - 1))[..., GRID_W - 1:]
    bias = jnp.where(col_in[None, None, :, None, :], z.transpose(0, 1, 3, 2, 4), NEG_INF)
    return bias.reshape(rpb.shape[0], kh, GRID_W, kh * GRID_W)


def _na_kernel(q_ref, k_ref, v_ref, kc_ref, vc_ref, bias_ref, o_ref, *, rb, rows, scale):
    blk = pl.program_id(2)
    kc = kc_ref[...]
    vc = vc_ref[...]
    win = NA_WIN_H * GRID_W
    starts, s_lat, s_ctx = [], [], []
    for j in range(rb):
        r = blk * rb + j
        r0 = jnp.clip(r - NA_WIN_H // 2, 0, rows - NA_WIN_H)
        start = pl.multiple_of(r0 * GRID_W, GRID_W)
        starts.append(start)
        q = q_ref[j * GRID_W:(j + 1) * GRID_W, :]
        s_lat.append(_dot_nt(q, k_ref[pl.ds(start, win), :]) * scale + bias_ref[0, r - r0])
        s_ctx.append(_dot_nt(q, kc) * scale)
    p_lat, p_ctx, dens = [], [], []
    for j in range(rb):
        m = jnp.maximum(jnp.max(s_lat[j], axis=-1, keepdims=True), jnp.max(s_ctx[j], axis=-1, keepdims=True))
        p_l = jnp.exp(s_lat[j] - m)
        p_c = jnp.exp(s_ctx[j] - m)
        dens.append(jnp.sum(p_l, axis=-1, keepdims=True) + jnp.sum(p_c, axis=-1, keepdims=True))
        p_lat.append(p_l.astype(BF16))
        p_ctx.append(p_c.astype(BF16))
    for j in range(rb):
        o = _dot(p_lat[j], v_ref[pl.ds(starts[j], win), :]) + _dot(p_ctx[j], vc)
        o_ref[j * GRID_W:(j + 1) * GRID_W, :] = (o / dens[j]).astype(o_ref.dtype)


def _na_attention(p, bias_tab, *, off, batch, seq, ctx):
    rows = seq // GRID_W
    assert rows >= NA_WIN_H
    rb = 16 if rows % 16 == 0 else 1
    dh = NA_HEAD_DIM
    nblk = rows // rb
    qc, kc_, vc_ = off['qa'] // dh, off['ka'] // dh, off['va'] // dh
    ctx_blk0 = batch * seq // ctx
    win = NA_WIN_H * GRID_W
    return pl.pallas_call(
        functools.partial(_na_kernel, rb=rb, rows=rows, scale=dh ** -0.5),
        name="na_attention",
        grid=(batch, NA_HEADS, nblk),
        in_specs=[pl.BlockSpec((rb * GRID_W, dh), lambda b, h, r: (b * nblk + r, qc + h)),
                  pl.BlockSpec((seq, dh), lambda b, h, r: (b, kc_ + h)),
                  pl.BlockSpec((seq, dh), lambda b, h, r: (b, vc_ + h)),
                  pl.BlockSpec((ctx, dh), lambda b, h, r: (ctx_blk0 + b, kc_ + h)),
                  pl.BlockSpec((ctx, dh), lambda b, h, r: (ctx_blk0 + b, vc_ + h)),
                  pl.BlockSpec((1, NA_WIN_H, GRID_W, win), lambda b, h, r: (h, 0, 0, 0))],
        out_specs=pl.BlockSpec((rb * GRID_W, dh), lambda b, h, r: (b * nblk + r, h)),
        out_shape=jax.ShapeDtypeStruct((batch * seq, NA_HEADS * dh), BF16),
        compiler_params=_params(("parallel", "parallel", "arbitrary")),
    )(p, p, p, p, p, bias_tab)


def _ctx_na_kernel(q_ref, k_ref, v_ref, o_ref, *, scale):
    s = _dot_nt(q_ref[...], k_ref[...]) * scale
    m = jnp.max(s, axis=-1, keepdims=True)
    e = jnp.exp(s - m)
    den = jnp.sum(e, axis=-1, keepdims=True)
    o_ref[...] = (_dot(e.astype(BF16), v_ref[...]) / den).astype(o_ref.dtype)


def _ctx_na_attention(p, *, off, batch, seq, ctx):
    dh = NA_HEAD_DIM
    qc, kc_, vc_ = off['qa'] // dh, off['ka'] // dh, off['va'] // dh
    blk0 = batch * seq // ctx
    return pl.pallas_call(
        functools.partial(_ctx_na_kernel, scale=dh ** -0.5),
        name="ctx_na_attention",
        grid=(batch, NA_HEADS),
        in_specs=[pl.BlockSpec((ctx, dh), lambda b, h: (blk0 + b, qc + h)),
                  pl.BlockSpec((ctx, dh), lambda b, h: (blk0 + b, kc_ + h)),
                  pl.BlockSpec((ctx, dh), lambda b, h: (blk0 + b, vc_ + h))],
        out_specs=pl.BlockSpec((ctx, dh), lambda b, h: (b, h)),
        out_shape=jax.ShapeDtypeStruct((batch * ctx, NA_HEADS * dh), BF16),
        compiler_params=_params(("parallel", "parallel")),
    )(p, p, p)


def _split_maps(q):
    lane = lax.broadcasted_iota(jnp.int32, q.shape, 1)
    zero = jnp.zeros_like(q)
    return jnp.concatenate([jnp.where(lane < DIFF_QK_DIM, q, zero), jnp.where(lane >= DIFF_QK_DIM, q, zero)],
                           axis=0)


def _diff_finish(o1, o2, par_ref):
    lam = par_ref[1:2, :]
    o = o1 - lam * o2
    y = o * lax.rsqrt(jnp.mean(o * o, axis=-1, keepdims=True) + SUBLN_EPS)
    return y * par_ref[0:1, :]


ONES_ROWS = 16


def _diff_kernel(q_ref, k_ref, v_ref, kc_ref, vc_ref, par_ref, o_ref, vt_ref, vct_ref, s_ref, *, tk):
    tq = q_ref.shape[0]
    dv = v_ref.shape[1]
    n_chunks = k_ref.shape[0] // tk

    @pl.when(pl.program_id(2) == 0)
    def _():
        for c in range(n_chunks):
            vt_ref[c, 0:dv, :] = v_ref[c * tk:(c + 1) * tk, :].astype(F32).T.astype(BF16)
            vt_ref[c, dv:, :] = jnp.ones((ONES_ROWS, tk), BF16)
        vct_ref[0:dv, :] = vc_ref[...].astype(F32).T.astype(BF16)
        vct_ref[dv:, :] = jnp.ones((ONES_ROWS, vct_ref.shape[1]), BF16)

    q_t = q_ref[...].astype(F32).T
    row = lax.broadcasted_iota(jnp.int32, q_t.shape, 0)
    qm = jnp.concatenate([jnp.where(row < DIFF_QK_DIM, q_t, 0.0), jnp.where(row >= DIFF_QK_DIM, q_t, 0.0)],
                         axis=1).astype(BF16)

    def scores(c):
        start = pl.multiple_of(c * tk, tk)
        return _dot(k_ref[pl.ds(start, tk), :], qm)

    def consume(s, vtb, carry):
        m, acc = carry
        m_new = jnp.maximum(m, jnp.max(s, axis=0, keepdims=True))
        alpha = jnp.exp2(m - m_new)
        e = jnp.exp2(s - m_new).astype(BF16)
        return m_new, alpha * acc + _dot(vtb, e)

    per_trip = 8 if n_chunks % 8 == 0 else 2

    def body(j, carry):
        c0 = per_trip * j
        for t in range(per_trip):
            s_ref[(t + 1) % 2] = scores(jnp.minimum(c0 + t + 1, n_chunks - 1))
            carry = consume(s_ref[t % 2], vt_ref[c0 + t], carry)
        return carry

    init = (jnp.full((1, 2 * tq), -jnp.inf, F32), jnp.zeros((dv + ONES_ROWS, 2 * tq), F32))
    s_ref[0] = scores(0)
    carry = lax.fori_loop(0, n_chunks // per_trip, body, init)
    m, acc = consume(_dot(kc_ref[...], qm), vct_ref[...], carry)
    o = acc[0:dv] / acc[dv:dv + 1]
    o_ref[...] = _diff_finish(o[:, :tq].T, o[:, tq:].T, par_ref).astype(o_ref.dtype)


def _diff_attention(qr, kr, p, par, *, off, batch, seq, ctx):
    dq2 = 2 * DIFF_QK_DIM
    dv = DIFF_V_DIM
    assert dq2 == LANES and dv == LANES
    tq = _tile(seq, 512)
    tk = _tile(seq, 512)
    assert (seq // tk) % 2 == 0
    nq = seq // tq
    kdc, vdc = off['kd'] // dq2, off['vd'] // dv
    ctx_blk0 = batch * seq // ctx
    return pl.pallas_call(
        functools.partial(_diff_kernel, tk=tk),
        name="diff_attention",
        scratch_shapes=[pltpu.VMEM((seq // tk, dv + ONES_ROWS, tk), BF16), pltpu.VMEM((dv + ONES_ROWS, ctx), BF16),
                        pltpu.VMEM((2, tk, 2 * tq), F32)],
        grid=(batch, DIFF_HEADS, nq),
        in_specs=[pl.BlockSpec((tq, dq2), lambda b, h, i: (b * nq + i, h)),
                  pl.BlockSpec((seq, dq2), lambda b, h, i: (b, h)),
                  pl.BlockSpec((seq, dv), lambda b, h, i: (b, vdc + h)),
                  pl.BlockSpec((ctx, dq2), lambda b, h, i: (ctx_blk0 + b, kdc + h)),
                  pl.BlockSpec((ctx, dv), lambda b, h, i: (ctx_blk0 + b, vdc + h)),
                  pl.BlockSpec((8, dv), lambda b, h, i: (0, 0))],
        out_specs=pl.BlockSpec((tq, dv), lambda b, h, i: (b * nq + i, h)),
        out_shape=jax.ShapeDtypeStruct((batch * seq, DIFF_HEADS * dv), BF16),
        compiler_params=_params(("parallel", "parallel", "arbitrary")),
    )(qr, kr, p, p, p, par)


def _ctx_diff_kernel(q_ref, k_ref, v_ref, par_ref, o_ref, *, scale):
    n = q_ref.shape[0]
    s = _dot_nt(_split_maps(q_ref[...]), k_ref[...]) * scale
    m = jnp.max(s, axis=-1, keepdims=True)
    e = jnp.exp(s - m)
    den = jnp.sum(e, axis=-1, keepdims=True)
    o = _dot(e.astype(BF16), v_ref[...]) / den
    o_ref[...] = _diff_finish(o[:n], o[n:], par_ref).astype(o_ref.dtype)


def _ctx_diff_attention(p, par, *, off, batch, seq, ctx):
    dq2 = 2 * DIFF_QK_DIM
    dv = DIFF_V_DIM
    qdc, kdc, vdc = off['qd'] // dq2, off['kd'] // dq2, off['vd'] // dv
    blk0 = batch * seq // ctx
    return pl.pallas_call(
        functools.partial(_ctx_diff_kernel, scale=DIFF_QK_DIM ** -0.5),
        name="ctx_diff_attention",
        grid=(batch, DIFF_HEADS),
        in_specs=[pl.BlockSpec((ctx, dq2), lambda b, h: (blk0 + b, qdc + h)),
                  pl.BlockSpec((ctx, dq2), lambda b, h: (blk0 + b, kdc + h)),
                  pl.BlockSpec((ctx, dv), lambda b, h: (blk0 + b, vdc + h)),
                  pl.BlockSpec((8, dv), lambda b, h: (0, 0))],
        out_specs=pl.BlockSpec((ctx, dv), lambda b, h: (b, h)),
        out_shape=jax.ShapeDtypeStruct((batch * ctx, DIFF_HEADS * dv), BF16),
        compiler_params=_params(("parallel", "parallel")),
    )(p, p, p, par)


def _ret_tables(log_gamma, reverse):
    cs = RET_CHUNK
    k_scale = RET_QK_DIM ** -0.5
    pos = jnp.arange(cs, dtype=F32)
    lg = log_gamma[:, None]
    rel = pos[:, None] - pos[None, :]
    if reverse:
        rel = -rel
        q_decay = jnp.exp(lg * (cs - pos))
        k_decay = jnp.exp(lg * pos)
    else:
        q_decay = jnp.exp(lg * (pos + 1.0))
        k_decay = jnp.exp(lg * (cs - 1.0 - pos))
    intra = jnp.where(rel >= 0, jnp.exp(lg[:, :, None] * jnp.maximum(rel, 0.0)), 0.0) * k_scale
    chunk_decay = jnp.broadcast_to(jnp.exp(lg * cs)[:, :, None], (RET_HEADS, 1, RET_V_DIM))
    return intra, q_decay[:, :, None], (k_decay * k_scale)[:, :, None], chunk_decay


def _ret_kernel(*refs, nh, final, sub_order):
    q_refs, k_refs, v_refs = refs[0:nh], refs[nh:2 * nh], refs[2 * nh:3 * nh]
    pos = 3 * nh
    if final:
        g_refs = refs[pos:pos + nh]
        of_ref, gain_ref = refs[pos + nh:pos + nh + 2]
        pos += nh + 2
    intra_ref, qdec_ref, kdec_ref, cd_ref, o_ref, s_ref = refs[pos:pos + 6]
    dv = RET_V_DIM

    @pl.when(pl.program_id(1) == 0)
    def _():
        s_ref[...] = jnp.zeros_like(s_ref)

    cs = RET_CHUNK
    for sub in sub_order:
        rows = slice(sub * cs, (sub + 1) * cs)
        atts, outs = [], []
        for h in range(nh):
            atts.append((_dot_nt(q_refs[h][rows, :], k_refs[h][rows, :]) * intra_ref[h]).astype(BF16))
        for h in range(nh):
            qd = (q_refs[h][rows, :].astype(F32) * qdec_ref[h]).astype(BF16)
            outs.append(_dot(atts[h], v_refs[h][rows, :]) + _dot(qd, s_ref[h].astype(BF16)))
        for h in range(nh):
            kd_t = (k_refs[h][rows, :].astype(F32) * kdec_ref[h]).T.astype(BF16)
            s_ref[h] = s_ref[h] * cd_ref[h] + _dot(kd_t, v_refs[h][rows, :])
        for h in range(nh):
            o = outs[h]
            cols = slice(h * dv, (h + 1) * dv)
            if final:
                o = o + of_ref[rows, cols]
                mu = jnp.mean(o, axis=-1, keepdims=True)
                var = jnp.mean(jnp.square(o - mu), axis=-1, keepdims=True)
                y = (o - mu) * lax.rsqrt(var + NORM_EPS) * gain_ref[:, cols]
                o_ref[rows, cols] = (_silu(g_refs[h][rows, :].astype(F32)) * y).astype(o_ref.dtype)
            else:
                o_ref[rows, cols] = o


def _retention(p, tabs_f, tabs_b, gain, *, off, batch, seq, ctx, n_tok):
    cs, nh, dk, dv = RET_CHUNK, RET_HEADS, RET_QK_DIM, RET_V_DIM
    per = 2 if (ctx // cs) % 2 == 0 and (seq // cs) % 2 == 0 else 1
    blk = per * cs
    lc, sc = ctx // blk, seq // blk
    ctx_blk0 = batch * seq // blk
    vw = nh * dv
    assert off['qb'] % dk == 0 and off['kb'] % dk == 0 and off['vb'] % dv == 0 and off['gb'] % dv == 0

    def chunk_fwd(b, t):
        return jnp.where(t < lc, ctx_blk0 + b * lc + t, b * sc + (t - lc))

    def chunk_bwd(b, t):
        return jnp.where(t < lc, ctx_blk0 + b * lc + (lc - 1 - t), b * sc + (sc - 1 - (t - lc)))

    def run(chunk, tabs, final, of):
        def head_spec(width, col0, h):
            return pl.BlockSpec((blk, width), lambda b, t: (chunk(b, t), col0 + h))

        in_specs = ([head_spec(dk, off['qb'] // dk, h) for h in range(nh)]
                    + [head_spec(dk, off['kb'] // dk, h) for h in range(nh)]
                    + [head_spec(dv, off['vb'] // dv, h) for h in range(nh)])
        args = [p] * (3 * nh)
        if final:
            in_specs += [head_spec(dv, off['gb'] // dv, h) for h in range(nh)]
            in_specs += [pl.BlockSpec((blk, vw), lambda b, t: (chunk(b, t), 0)),
                         pl.BlockSpec((1, vw), lambda b, t: (0, 0))]
            args += [p] * nh + [of, gain.reshape(1, vw).astype(F32)]
        in_specs += [pl.BlockSpec((nh, cs, cs), lambda b, t: (0, 0, 0)),
                     pl.BlockSpec((nh, cs, 1), lambda b, t: (0, 0, 0)),
                     pl.BlockSpec((nh, cs, 1), lambda b, t: (0, 0, 0)),
                     pl.BlockSpec((nh, 1, dv), lambda b, t: (0, 0, 0))]
        args += list(tabs)
        return pl.pallas_call(
            functools.partial(_ret_kernel, nh=nh, final=final,
                              sub_order=tuple(reversed(range(per))) if final else tuple(range(per))),
            name="retention_bwd_norm_gate" if final else "retention_fwd",
            grid=(batch, lc + sc),
            in_specs=in_specs,
            out_specs=pl.BlockSpec((blk, vw), lambda b, t: (chunk(b, t), 0)),
            out_shape=jax.ShapeDtypeStruct((n_tok, vw), BF16 if final else F32),
            scratch_shapes=[pltpu.VMEM((nh, dk, dv), F32)],
            compiler_params=_params(("parallel", "arbitrary")),
        )(*args)

    o_f = run(chunk_fwd, tabs_f, False, None)
    return run(chunk_bwd, tabs_b, True, o_f)


def _merge_kernel(*refs, lat_tiles):
    oa_ref, ob_ref, od_ref = refs[0:3]
    oa = oa_ref[...]
    od = od_ref[...]
    pos = 3
    if lat_tiles is not None:
        is_ctx = pl.program_id(0) >= lat_tiles
        oa = jnp.where(is_ctx, refs[3][...], oa)
        od = jnp.where(is_ctx, refs[4][...], od)
        pos = 5
    ga_ref, gb_ref, gd_ref, wa_ref, wb_ref, wd_ref, o_ref = refs[pos:pos + 7]
    j = pl.program_id(1)
    y = jax.nn.sigmoid(ga_ref[...].astype(F32)) * _dot(oa, wa_ref[j])
    y += jax.nn.sigmoid(gb_ref[...].astype(F32)) * _dot(ob_ref[...], wb_ref[j])
    y += jax.nn.sigmoid(gd_ref[...].astype(F32)) * _dot(od, wd_ref[j])
    o_ref[...] = y.astype(o_ref.dtype)


def _col_blocked(w, tn):
    k, n = w.shape
    return w.reshape(k, n // tn, tn).transpose(1, 0, 2)


def _merge(oa, ob, od, ctx_pair, p, wa, wb, wd, *, off, n_rows, n_lat, tm):
    d = D_MODEL
    tn = _tile(d, 512)
    assert off['gates'] % tn == 0
    g0 = off['gates'] // tn
    nj = d // tn
    lat_tiles = n_lat // tm

    def lat(a):
        return pl.BlockSpec((tm, a.shape[1]), lambda i, j: (jnp.minimum(i, lat_tiles - 1), 0))

    def ctx_rows(a):
        return pl.BlockSpec((tm, a.shape[1]), lambda i, j: (jnp.maximum(i - lat_tiles, 0), 0))

    def wspec(w):
        return pl.BlockSpec((nj, w.shape[0], tn), lambda i, j: (0, 0, 0))

    def gspec(br):
        return pl.BlockSpec((tm, tn), lambda i, j: (i, g0 + br * nj + j))

    in_specs = [lat(oa), pl.BlockSpec((tm, ob.shape[1]), lambda i, j: (i, 0)), lat(od)]
    args = [oa, ob, od]
    if ctx_pair is not None:
        in_specs += [ctx_rows(ctx_pair[0]), ctx_rows(ctx_pair[1])]
        args += list(ctx_pair)
    in_specs += [gspec(0), gspec(1), gspec(2), wspec(wa), wspec(wb), wspec(wd)]
    args += [p, p, p, _col_blocked(wa, tn), _col_blocked(wb, tn), _col_blocked(wd, tn)]
    return pl.pallas_call(
        functools.partial(_merge_kernel, lat_tiles=lat_tiles if ctx_pair is not None else None),
        name="branch_merge",
        grid=(n_rows // tm, nj),
        in_specs=in_specs,
        out_specs=pl.BlockSpec((tm, tn), lambda i, j: (i, j)),
        out_shape=jax.ShapeDtypeStruct((n_rows, d), BF16),
        compiler_params=_params(("parallel", "arbitrary")),
    )(*args)


def _router_kernel(h_ref, mod_ref, g_ref, rw_ref, rb_ref, u_ref, idx_ref, wt_ref, *, row):
    mod = mod_ref[0]
    u = _normmod(h_ref[...], g_ref[...], mod[row:row + 1], mod[row + 1:row + 2])
    u_hi = u.astype(BF16)
    u_ref[...] = u_hi
    u_lo = (u - u_hi.astype(F32)).astype(BF16)
    w = rw_ref[...]
    w_hi = w.astype(BF16)
    w_lo = (w - w_hi.astype(F32)).astype(BF16)
    logits = _dot(u_hi, w_hi) + _dot(u_hi, w_lo) + _dot(u_lo, w_hi) + rb_ref[...]
    lane = lax.broadcasted_iota(jnp.int32, logits.shape, 1)
    lg = jnp.where(lane < N_EXPERTS, logits, -jnp.inf)
    v1 = jnp.max(lg, axis=-1, keepdims=True)
    lane_f = lane.astype(F32)
    i1 = jnp.min(jnp.where(lg == v1, lane_f, float(LANES)), axis=-1, keepdims=True).astype(jnp.int32)
    lg2 = jnp.where(lane == i1, -jnp.inf, lg)
    v2 = jnp.max(lg2, axis=-1, keepdims=True)
    i2 = jnp.min(jnp.where(lg2 == v2, lane_f, float(LANES)), axis=-1, keepdims=True).astype(jnp.int32)
    e = jnp.exp(v2 - v1)
    w1 = 1.0 / (1.0 + e)
    w2 = e / (1.0 + e)
    idx_ref[...] = jnp.where(lane == 0, i1, jnp.where(lane == 1, i2, 0))
    wt_ref[...] = jnp.where(lane == 0, w1, jnp.where(lane == 1, w2, 0.0))


def _router(h, mod, g, router_w, router_b, *, row, n_rows, tm, seq, batch):
    d = D_MODEL
    rw = jnp.zeros((d, LANES), F32).at[:, :N_EXPERTS].set(router_w.astype(F32))
    rb = jnp.zeros((1, LANES), F32).at[0, :N_EXPERTS].set(router_b.astype(F32))
    mod_idx = _mod_index(tm, seq, batch)
    return pl.pallas_call(
        functools.partial(_router_kernel, row=row),
        name="norm_router",
        grid=(n_rows // tm,),
        in_specs=[pl.BlockSpec((tm, d), lambda i: (i, 0)),
                  pl.BlockSpec((1, 8, d), lambda i: mod_idx(i, 0)),
                  pl.BlockSpec((1, d), lambda i: (0, 0)),
                  pl.BlockSpec((d, LANES), lambda i: (0, 0)),
                  pl.BlockSpec((1, LANES), lambda i: (0, 0))],
        out_specs=[pl.BlockSpec((tm, d), lambda i: (i, 0)),
                   pl.BlockSpec((tm, LANES), lambda i: (i, 0)),
                   pl.BlockSpec((tm, LANES), lambda i: (i, 0))],
        out_shape=[jax.ShapeDtypeStruct((n_rows, d), BF16),
                   jax.ShapeDtypeStruct((n_rows, LANES), jnp.int32),
                   jax.ShapeDtypeStruct((n_rows, LANES), F32)],
        compiler_params=_params(("parallel",)),
    )(h, mod, g.reshape(1, d), rw, rb)


def _route_plan(ridx, rwt, tme):
    n_tok = ridx.shape[0]
    n2 = n_tok * TOP_K
    n_pad = n2 + N_EXPERTS * tme
    n_tiles = n_pad // tme
    e_flat = ridx.reshape(-1)
    order = jnp.argsort(e_flat, stable=True).astype(jnp.int32)
    counts = jnp.sum(e_flat[:, None] == jnp.arange(N_EXPERTS)[None, :], axis=0).astype(jnp.int32)
    padded = ((counts + tme - 1) // tme) * tme
    pend = jnp.cumsum(padded)
    pstart = pend - padded
    ustart = jnp.cumsum(counts) - counts
    tile_start = jnp.arange(n_tiles, dtype=jnp.int32) * tme
    tile_expert = jnp.minimum(jnp.sum(tile_start[:, None] >= pend[None, :], axis=1), N_EXPERTS - 1).astype(jnp.int32)
    tile_valid = (tile_start < pend[-1]).astype(jnp.int32)
    slot = jnp.arange(n_pad, dtype=jnp.int32)
    slot_e = jnp.repeat(tile_expert, tme)
    within = slot - pstart[slot_e]
    valid = (within < counts[slot_e]) & (jnp.repeat(tile_valid, tme) > 0)
    src = jnp.clip(ustart[slot_e] + within, 0, n2 - 1)
    assign = order[src]
    perm_tok = jnp.where(valid, assign // TOP_K, 0)
    w_sorted = jnp.where(valid, rwt.reshape(-1)[assign], 0.0)
    e_sorted = e_flat[order]
    pos = jnp.arange(n2, dtype=jnp.int32) - ustart[e_sorted] + pstart[e_sorted]
    slot_of = pos[jnp.argsort(order)].reshape(n_tok, TOP_K)
    return perm_tok, w_sorted, slot_of, tile_expert, tile_valid


CAST_SLICES = 4


def _moe_up_kernel(te_ref, tv_ref, tf_ref, a_ref, w1_ref, w3_ref, o_ref, w1b_ref, w3b_ref):
    del te_ref
    i = pl.program_id(1)
    first = tf_ref[i] > 0
    valid = tv_ref[i] > 0

    @pl.when(first)
    def _():
        a = a_ref[...]
        tn = o_ref.shape[1]
        cw = tn // CAST_SLICES
        for c in range(CAST_SLICES):
            cols = slice(c * cw, (c + 1) * cw)
            w1c = w1_ref[0, :, cols].astype(BF16)
            w3c = w3_ref[0, :, cols].astype(BF16)
            w1b_ref[:, cols] = w1c
            w3b_ref[:, cols] = w3c
            y = _silu(_dot(a, w1c)) * _dot(a, w3c)
            o_ref[:, cols] = jnp.where(valid, y, 0.0).astype(o_ref.dtype)

    @pl.when(jnp.logical_and(jnp.logical_not(first), valid))
    def _():
        a = a_ref[...]
        o_ref[...] = (_silu(_dot(a, w1b_ref[...])) * _dot(a, w3b_ref[...])).astype(o_ref.dtype)

    @pl.when(jnp.logical_and(jnp.logical_not(first), jnp.logical_not(valid)))
    def _():
        o_ref[...] = jnp.zeros_like(o_ref)


def _moe_up(a, w1, w3, tile_expert, tile_valid, *, tme):
    n_pad, d = a.shape
    f = w1.shape[2]
    tn = _tile(f, 1024)
    tile_first = jnp.concatenate([jnp.ones((1,), jnp.int32),
                                  (tile_expert[1:] != tile_expert[:-1]).astype(jnp.int32)])
    return pl.pallas_call(
        _moe_up_kernel,
        name="moe_up",
        grid_spec=pltpu.PrefetchScalarGridSpec(
            num_scalar_prefetch=3,
            grid=(f // tn, n_pad // tme),
            in_specs=[pl.BlockSpec((tme, d), lambda j, i, te, tv, tf: (i, 0)),
                      pl.BlockSpec((1, d, tn), lambda j, i, te, tv, tf: (te[i], 0, j)),
                      pl.BlockSpec((1, d, tn), lambda j, i, te, tv, tf: (te[i], 0, j))],
            out_specs=pl.BlockSpec((tme, tn), lambda j, i, te, tv, tf: (i, j)),
            scratch_shapes=[pltpu.VMEM((d, tn), BF16), pltpu.VMEM((d, tn), BF16)]),
        out_shape=jax.ShapeDtypeStruct((n_pad, f), BF16),
        compiler_params=_params(("parallel", "arbitrary")),
    )(tile_expert, tile_valid, tile_first, a, w1, w3)


def _cast_kernel(x_ref, o_ref):
    o_ref[...] = x_ref[...].astype(o_ref.dtype)


def _cast_bf16(w):
    e, k, n = w.shape
    tk = _tile(k, 1024)
    return pl.pallas_call(
        _cast_kernel,
        name="expert_weight_cast",
        grid=(e, k // tk),
        in_specs=[pl.BlockSpec((1, tk, n), lambda i, j: (i, j, 0))],
        out_specs=pl.BlockSpec((1, tk, n), lambda i, j: (i, j, 0)),
        out_shape=jax.ShapeDtypeStruct(w.shape, BF16),
        compiler_params=_params(("parallel", "parallel")),
    )(w)


def _moe_down_kernel(te_ref, tv_ref, a_ref, w_ref, ws_ref, o_ref):
    del te_ref
    i = pl.program_id(1)

    @pl.when(tv_ref[i] > 0)
    def _():
        o_ref[...] = (ws_ref[...] * _dot(a_ref[...], w_ref[0])).astype(o_ref.dtype)

    @pl.when(tv_ref[i] == 0)
    def _():
        o_ref[...] = jnp.zeros_like(o_ref)


def _moe_down(a, w2, w_sorted, tile_expert, tile_valid, *, tme):
    n_pad, f = a.shape
    d = w2.shape[2]
    tn = _tile(d, 1024)
    return pl.pallas_call(
        _moe_down_kernel,
        name="moe_down",
        grid_spec=pltpu.PrefetchScalarGridSpec(
            num_scalar_prefetch=2,
            grid=(d // tn, n_pad // tme),
            in_specs=[pl.BlockSpec((tme, f), lambda j, i, te, tv: (i, 0)),
                      pl.BlockSpec((1, f, tn), lambda j, i, te, tv: (te[i], 0, j)),
                      pl.BlockSpec((tme, 1), lambda j, i, te, tv: (i, 0))],
            out_specs=pl.BlockSpec((tme, tn), lambda j, i, te, tv: (i, j))),
        out_shape=jax.ShapeDtypeStruct((n_pad, d), BF16),
        compiler_params=_params(("parallel", "arbitrary")),
    )(tile_expert, tile_valid, a, w2, w_sorted.reshape(n_pad, 1))


def _combine_norm_kernel(h_ref, y1_ref, y2_ref, mod_ref, g_ref, o_ref, *, row):
    gate = mod_ref[0][row:row + 1]
    x = h_ref[...] + gate * (y1_ref[...].astype(F32) + y2_ref[...].astype(F32))
    ms = jnp.mean(x * x, axis=-1, keepdims=True)
    o_ref[...] = x * lax.rsqrt(ms + NORM_EPS) * g_ref[...]


def _combine_norm(h, y12, mod, g, *, row, n_rows, tm, seq, batch):
    d = D_MODEL
    mod_idx = _mod_index(tm, seq, batch)
    rowspec = pl.BlockSpec((tm, d), lambda i: (i, 0))
    nt = n_rows // tm
    return pl.pallas_call(
        functools.partial(_combine_norm_kernel, row=row),
        name="moe_combine_final_norm",
        grid=(nt,),
        in_specs=[rowspec, rowspec, pl.BlockSpec((tm, d), lambda i: (nt + i, 0)),
                  pl.BlockSpec((1, 8, d), lambda i: mod_idx(i, 0)),
                  pl.BlockSpec((1, d), lambda i: (0, 0))],
        out_specs=rowspec,
        out_shape=jax.ShapeDtypeStruct((n_rows, d), F32),
        compiler_params=_params(("parallel",)),
    )(h, y12, y12, mod, g.reshape(1, d))


def _token_mixers(h, mod, lp, layer_idx, rope_tabs, *, dims, need_ctx):
    batch, seq, ctx, n_tok, n_lat, tm = dims
    off, in_w = _layout()
    bf = lambda a: a.astype(BF16)
    p = _normmod_mm(h, mod, lp['norm1'], bf(lp['w_in']), row=0, n_rows=n_tok, tm=WIDE_ROW_FACTOR * tm, seq=seq,
                    batch=batch, tn_pref=768)

    bias_tab = _na_bias_table(lp['na_rpb'], seq // GRID_W)
    oa = _na_attention(p, bias_tab, off=off, batch=batch, seq=seq, ctx=ctx)

    log_gamma = jax.nn.log_sigmoid(lp['ret_decay'].astype(F32))
    ob = _retention(p, _ret_tables(log_gamma[0], False), _ret_tables(log_gamma[1], True), lp['ret_gn'],
                    off=off, batch=batch, seq=seq, ctx=ctx, n_tok=n_tok)

    lam_init = 0.8 - 0.6 * math.exp(-0.3 * layer_idx)
    lq1, lk1, lq2, lk2 = lp['diff_lam'].astype(F32)
    lam = jnp.exp(jnp.sum(lq1 * lk1)) - jnp.exp(jnp.sum(lq2 * lk2)) + lam_init
    par = jnp.zeros((8, DIFF_V_DIM), F32)
    par = par.at[0].set(lp['diff_subln'].astype(F32) * (1.0 - lam_init)).at[1].set(lam)
    qr, kr = _rope(p, rope_tabs, off=off, n_lat=n_lat, seq=seq, tm=tm)
    od = _diff_attention(qr, kr, p, par, off=off, batch=batch, seq=seq, ctx=ctx)

    ctx_pair = None
    if need_ctx:
        ctx_pair = (_ctx_na_attention(p, off=off, batch=batch, seq=seq, ctx=ctx),
                    _ctx_diff_attention(p, par, off=off, batch=batch, seq=seq, ctx=ctx))
    n_rows = n_tok if need_ctx else n_lat
    ymid = _merge(oa, ob, od, ctx_pair, p, bf(lp['w_br_a']), bf(lp['w_br_b']), bf(lp['w_br_c']), off=off,
                  n_rows=n_rows, n_lat=n_lat, tm=tm)
    return _mm_res(ymid, bf(lp['w_out']), h, mod, row=2, n_rows=n_rows, tm=tm, seq=seq, batch=batch,
                   weight_tile_bytes=8 << 20)


def _forward(x, c, ctx_tok, c_ctx, layers, final_norm):
    batch, seq, d = x.shape
    ctx = ctx_tok.shape[1]
    n_lat = batch * seq
    n_tok = n_lat + batch * ctx
    tm = min(ROW_TILE, batch * ctx)
    assert d == D_MODEL and seq % tm == 0 and (batch * ctx) % tm == 0 and seq % GRID_W == 0
    assert seq % RET_CHUNK == 0 and ctx % RET_CHUNK == 0 and n_lat % ctx == 0 and batch + 1 <= 8
    dims = (batch, seq, ctx, n_tok, n_lat, tm)
    bf = lambda a: a.astype(BF16)

    h = jnp.concatenate([x.reshape(n_lat, d), ctx_tok.reshape(batch * ctx, d)], axis=0).astype(F32)
    cond8 = jnp.zeros((8, d), F32).at[:batch].set(c).at[batch].set(c_ctx)
    rope_tabs = _rope_tables(seq)
    n_layers = len(layers)
    out = None
    for li, lp in enumerate(layers):
        need_ctx = li < n_layers - 1
        m = _adaln(cond8, lp['w_ada'], lp['b_ada'])
        mod = jnp.zeros((batch + 1, 8, d), F32).at[:, :6].set(m[:batch + 1].reshape(batch + 1, 6, d))
        h = _token_mixers(h, mod, lp, li, rope_tabs, dims=dims, need_ctx=need_ctx)
        n_rows = n_tok if need_ctx else n_lat
        if 'ffn_w1' in lp:
            a = _normmod_swiglu(h, mod, lp['norm2'], bf(lp['ffn_w1']), bf(lp['ffn_w3']), row=3, n_rows=n_rows,
                                tm=WIDE_ROW_FACTOR * tm, seq=seq, batch=batch)
            h = _mm_res(a, bf(lp['ffn_w2']), h, mod, row=5, n_rows=n_rows, tm=WIDE_ROW_FACTOR * tm, seq=seq,
                        batch=batch, weight_tile_bytes=3 << 20)
        else:
            assert not need_ctx and li == n_layers - 1
            tme = min(MOE_ROW_TILE, n_lat)
            u, ridx, rwt = _router(h, mod, lp['norm2'], lp['router_w'], lp['router_b'], row=3, n_rows=n_lat,
                                   tm=tm, seq=seq, batch=batch)
            perm_tok, w_sorted, slot_of, tile_expert, tile_valid = _route_plan(ridx[:, :TOP_K], rwt[:, :TOP_K], tme)
            u_sorted = u.at[perm_tok].get(mode='promise_in_bounds')
            a = _moe_up(u_sorted, lp['exp_w1'], lp['exp_w3'], tile_expert, tile_valid, tme=tme)
            y = _moe_down(a, _cast_bf16(lp['exp_w2']), w_sorted, tile_expert, tile_valid, tme=tme)
            y12 = y.at[slot_of.T.reshape(-1)].get(mode='promise_in_bounds')
            out = _combine_norm(h, y12, mod, final_norm, row=5, n_rows=n_lat, tm=tm, seq=seq, batch=batch)
    return out.reshape(batch, seq, d)


def kernel(x, c, ctx, c_ctx, l0_w_ada, l0_b_ada, l0_norm1, l0_w_in, l0_na_rpb, l0_ret_decay, l0_ret_gn, l0_diff_lam, l0_diff_subln, l0_w_br_a, l0_w_br_b, l0_w_br_c, l0_w_out, l0_norm2, l0_ffn_w1, l0_ffn_w3, l0_ffn_w2, l1_w_ada, l1_b_ada, l1_norm1, l1_w_in, l1_na_rpb, l1_ret_decay, l1_ret_gn, l1_diff_lam, l1_diff_subln, l1_w_br_a, l1_w_br_b, l1_w_br_c, l1_w_out, l1_norm2, l1_router_w, l1_router_b, l1_exp_w1, l1_exp_w3, l1_exp_w2, final_norm):
    layers = (
        dict(w_ada=l0_w_ada, b_ada=l0_b_ada, norm1=l0_norm1, w_in=l0_w_in, na_rpb=l0_na_rpb,
             ret_decay=l0_ret_decay, ret_gn=l0_ret_gn, diff_lam=l0_diff_lam, diff_subln=l0_diff_subln,
             w_br_a=l0_w_br_a, w_br_b=l0_w_br_b, w_br_c=l0_w_br_c, w_out=l0_w_out, norm2=l0_norm2,
             ffn_w1=l0_ffn_w1, ffn_w3=l0_ffn_w3, ffn_w2=l0_ffn_w2),
        dict(w_ada=l1_w_ada, b_ada=l1_b_ada, norm1=l1_norm1, w_in=l1_w_in, na_rpb=l1_na_rpb,
             ret_decay=l1_ret_decay, ret_gn=l1_ret_gn, diff_lam=l1_diff_lam, diff_subln=l1_diff_subln,
             w_br_a=l1_w_br_a, w_br_b=l1_w_br_b, w_br_c=l1_w_br_c, w_out=l1_w_out, norm2=l1_norm2,
             router_w=l1_router_w, router_b=l1_router_b, exp_w1=l1_exp_w1, exp_w3=l1_exp_w3,
             exp_w2=l1_exp_w2),
    )
    return _forward(x, c, ctx, c_ctx, layers, final_norm)
```

```python
import functools
import math

import jax
import jax.numpy as jnp
from jax import lax
from jax.experimental import pallas as pl
from jax.experimental.pallas import tpu as pltpu

D_MODEL = 2048
GRID_W = 64
NA_HEADS = 6
NA_HEAD_DIM = 128
NA_WIN_H = 8
NA_WIN_W = 16
RET_HEADS = 4
RET_QK_DIM = 128
RET_V_DIM = 256
RET_CHUNK = 128
DIFF_HEADS = 6
DIFF_QK_DIM = 64
DIFF_V_DIM = 128
ROPE_BASE = 10000.0
N_EXPERTS = 8
TOP_K = 2
N_BRANCHES = 3
NORM_EPS = 1e-6
SUBLN_EPS = 1e-5
NEG_INF = -1e30

LANES = 128
ROW_TILE = 512
WIDE_ROW_FACTOR = 2
MOE_ROW_TILE = 512
VMEM_LIMIT = 56 << 20

F32 = jnp.float32
BF16 = jnp.bfloat16


def _params(sem, vmem=VMEM_LIMIT):
    return pltpu.CompilerParams(dimension_semantics=sem, vmem_limit_bytes=vmem)


def _dot(a, b):
    return jnp.dot(a, b, preferred_element_type=F32)


def _dot_nt(a, b):
    return lax.dot_general(a, b, (((1,), (1,)), ((), ())), preferred_element_type=F32)


def _silu(x):
    return x * jax.nn.sigmoid(x)


def _tile(n, pref):
    if n <= pref:
        return n
    t = (pref // LANES) * LANES
    while t >= LANES:
        if n % t == 0:
            return t
        t -= LANES
    raise ValueError("no lane-aligned tile for %d" % n)


def _layout():
    na_w = NA_HEADS * NA_HEAD_DIM
    ret_qk_w = RET_HEADS * RET_QK_DIM
    ret_v_w = RET_HEADS * RET_V_DIM
    diff_qk_w = DIFF_HEADS * 2 * DIFF_QK_DIM
    diff_v_w = DIFF_HEADS * DIFF_V_DIM
    names = ('qa', 'ka', 'va', 'qb', 'kb', 'vb', 'gb', 'qd', 'kd', 'vd', 'gates')
    widths = (na_w, na_w, na_w, ret_qk_w, ret_qk_w, ret_v_w, ret_v_w, diff_qk_w, diff_qk_w, diff_v_w,
              N_BRANCHES * D_MODEL)
    off = {}
    acc = 0
    for n, w in zip(names, widths):
        off[n] = acc
        acc += w
    return off, acc


def _normmod(x, g, shift, scale):
    ms = jnp.mean(x * x, axis=-1, keepdims=True)
    y = x * lax.rsqrt(ms + NORM_EPS) * g
    return y * (1.0 + scale) + shift


def _ada_kernel(c_ref, w_ref, b_ref, o_ref):
    a = _silu(c_ref[...]).astype(BF16)
    o_ref[...] = _dot(a, w_ref[...].astype(BF16)) + b_ref[...]


def _adaln(cond8, w_ada, b_ada):
    d, n = w_ada.shape
    tn = _tile(n, 1024)
    return pl.pallas_call(
        _ada_kernel,
        name="adaln",
        grid=(n // tn,),
        in_specs=[pl.BlockSpec((8, d), lambda j: (0, 0)),
                  pl.BlockSpec((d, tn), lambda j: (0, j)),
                  pl.BlockSpec((1, tn), lambda j: (0, j))],
        out_specs=pl.BlockSpec((8, tn), lambda j: (0, j)),
        out_shape=jax.ShapeDtypeStruct((8, n), F32),
        compiler_params=_params(("parallel",)),
    )(cond8, w_ada, b_ada.reshape(1, n))


def _normmod_mm_kernel(h_ref, mod_ref, g_ref, w_ref, o_ref, u_ref, *, row):
    @pl.when(pl.program_id(1) == 0)
    def _():
        mod = mod_ref[0]
        u = _normmod(h_ref[...], g_ref[...], mod[row:row + 1], mod[row + 1:row + 2])
        u_ref[...] = u.astype(BF16)

    o_ref[...] = _dot(u_ref[...], w_ref[...]).astype(o_ref.dtype)


def _normmod_swiglu_kernel(h_ref, mod_ref, g_ref, w1_ref, w3_ref, o_ref, u_ref, *, row):
    @pl.when(pl.program_id(1) == 0)
    def _():
        mod = mod_ref[0]
        u = _normmod(h_ref[...], g_ref[...], mod[row:row + 1], mod[row + 1:row + 2])
        u_ref[...] = u.astype(BF16)

    u = u_ref[...]
    o_ref[...] = (_silu(_dot(u, w1_ref[...])) * _dot(u, w3_ref[...])).astype(o_ref.dtype)


def _mod_index(tm, seq, batch):
    return lambda i, j: (jnp.minimum((i * tm) // seq, batch), 0, 0)


def _normmod_mm(h, mod, g, w, *, row, n_rows, tm, seq, batch, tn_pref):
    d, n = w.shape
    tn = _tile(n, tn_pref)
    return pl.pallas_call(
        functools.partial(_normmod_mm_kernel, row=row),
        name="norm_in_proj",
        grid=(pl.cdiv(n_rows, tm), n // tn),
        in_specs=[pl.BlockSpec((tm, d), lambda i, j: (i, 0)),
                  pl.BlockSpec((1, 8, d), _mod_index(tm, seq, batch)),
                  pl.BlockSpec((1, d), lambda i, j: (0, 0)),
                  pl.BlockSpec((d, tn), lambda i, j: (0, j))],
        out_specs=pl.BlockSpec((tm, tn), lambda i, j: (i, j)),
        out_shape=jax.ShapeDtypeStruct((h.shape[0], n), BF16),
        scratch_shapes=[pltpu.VMEM((tm, d), BF16)],
        compiler_params=_params(("parallel", "arbitrary")),
    )(h, mod, g.reshape(1, d), w)


def _normmod_swiglu(h, mod, g, w1, w3, *, row, n_rows, tm, seq, batch):
    d, n = w1.shape
    tn = _tile(n, 512)
    return pl.pallas_call(
        functools.partial(_normmod_swiglu_kernel, row=row),
        name="norm_swiglu_up",
        grid=(pl.cdiv(n_rows, tm), n // tn),
        in_specs=[pl.BlockSpec((tm, d), lambda i, j: (i, 0)),
                  pl.BlockSpec((1, 8, d), _mod_index(tm, seq, batch)),
                  pl.BlockSpec((1, d), lambda i, j: (0, 0)),
                  pl.BlockSpec((d, tn), lambda i, j: (0, j)),
                  pl.BlockSpec((d, tn), lambda i, j: (0, j))],
        out_specs=pl.BlockSpec((tm, tn), lambda i, j: (i, j)),
        out_shape=jax.ShapeDtypeStruct((h.shape[0], n), BF16),
        scratch_shapes=[pltpu.VMEM((tm, d), BF16)],
        compiler_params=_params(("parallel", "arbitrary")),
    )(h, mod, g.reshape(1, d), w1, w3)


def _mm_res_kernel(a_ref, w_ref, h_ref, mod_ref, o_ref, *, row):
    gate = mod_ref[0][row:row + 1]
    o_ref[...] = h_ref[...] + gate * _dot(a_ref[...], w_ref[...])


def _mm_res(a, w, h, mod, *, row, n_rows, tm, seq, batch, weight_tile_bytes):
    kdim, n = w.shape
    tn = _tile(n, weight_tile_bytes // (2 * kdim))
    mod_idx = _mod_index(tm, seq, batch)
    return pl.pallas_call(
        functools.partial(_mm_res_kernel, row=row),
        name="proj_gated_residual",
        grid=(pl.cdiv(n_rows, tm), n // tn),
        in_specs=[pl.BlockSpec((tm, kdim), lambda i, j: (i, 0)),
                  pl.BlockSpec((kdim, tn), lambda i, j: (0, j)),
                  pl.BlockSpec((tm, tn), lambda i, j: (i, j)),
                  pl.BlockSpec((1, 8, tn), lambda i, j: (mod_idx(i, j)[0], 0, j))],
        out_specs=pl.BlockSpec((tm, tn), lambda i, j: (i, j)),
        out_shape=jax.ShapeDtypeStruct((n_rows, n), F32),
        compiler_params=_params(("parallel", "arbitrary")),
    )(a, w, h, mod)


def _rope_tables(seq):
    t = jnp.arange(seq)
    row = (t // GRID_W).astype(F32)
    col = (t % GRID_W).astype(F32)
    axis_dim = DIFF_QK_DIM // 2
    inv_freq = ROPE_BASE ** (-jnp.arange(0, axis_dim, 2, dtype=F32) / axis_dim)
    ar = row[:, None] * inv_freq
    ac = col[:, None] * inv_freq
    ang = jnp.concatenate([ar, ar, ac, ac], axis=-1)
    reps = LANES // DIFF_QK_DIM
    cos = jnp.tile(jnp.cos(ang), (1, reps))
    sin = jnp.tile(jnp.sin(ang), (1, reps))
    half = DIFF_QK_DIM // 4
    first = (jnp.arange(LANES) % (2 * half)) < half
    sin_dn = jnp.where(first, 0.0, sin)
    sin_up = jnp.where(first, -sin, 0.0)
    return cos, sin_dn, sin_up


def _rope_kernel(q_ref, k_ref, cos_ref, sdn_ref, sup_ref, qo_ref, ko_ref, *, q_scale):
    cos = cos_ref[...]
    sdn = sdn_ref[...]
    sup = sup_ref[...]
    half = DIFF_QK_DIM // 4
    for src, dst, mul in ((q_ref, qo_ref, q_scale), (k_ref, ko_ref, None)):
        for c in range(src.shape[1] // LANES):
            x = src[:, c * LANES:(c + 1) * LANES].astype(F32)
            y = x * cos + pltpu.roll(x, half, 1) * sdn + pltpu.roll(x, LANES - half, 1) * sup
            if mul is not None:
                y = y * mul
            dst[:, c * LANES:(c + 1) * LANES] = y.astype(dst.dtype)


def _rope(p, tabs, *, off, n_lat, seq, tm):
    w = DIFF_HEADS * 2 * DIFF_QK_DIM
    assert off['qd'] % w == 0 and off['kd'] % w == 0
    per = seq // tm
    tab_spec = pl.BlockSpec((tm, LANES), lambda i: (i % per, 0))
    return pl.pallas_call(
        functools.partial(_rope_kernel, q_scale=DIFF_QK_DIM ** -0.5 * math.log2(math.e)),
        name="rope",
        grid=(n_lat // tm,),
        in_specs=[pl.BlockSpec((tm, w), lambda i: (i, off['qd'] // w)),
                  pl.BlockSpec((tm, w), lambda i: (i, off['kd'] // w)),
                  tab_spec, tab_spec, tab_spec],
        out_specs=[pl.BlockSpec((tm, w), lambda i: (i, 0)), pl.BlockSpec((tm, w), lambda i: (i, 0))],
        out_shape=[jax.ShapeDtypeStruct((n_lat, w), BF16), jax.ShapeDtypeStruct((n_lat, w), BF16)],
        compiler_params=_params(("parallel",)),
    )(p, p, *tabs)


def _na_bias_table(rpb, rows):
    del rows
    kh = NA_WIN_H
    cols = jnp.arange(GRID_W)
    c0 = jnp.clip(cols - NA_WIN_W // 2, 0, GRID_W - NA_WIN_W)
    col_in = (cols[None, :] >= c0[:, None]) & (cols[None, :] < c0[:, None] + NA_WIN_W)
    rpb = rpb.astype(F32)
    by_row = jnp.stack([rpb[:, kh - 1 - o:2 * kh - 1 - o, :] for o in range(kh)], axis=1)
    edge = GRID_W - NA_WIN_W
    period = 2 * GRID_W
    ext = jnp.concatenate([jnp.repeat(by_row[..., :1], edge, axis=-1), by_row,
                           jnp.repeat(by_row[..., -1:], edge + 1, axis=-1)], axis=-1)
    z = jnp.tile(ext, (1, 1, 1, GRID_W))[..., :GRID_W * (period - 1)]
    z = z.reshape(ext.shape[:3] + (GRID_W, period - 1))[..., GRID_W - 1:]
    bias = jnp.where(col_in[None, None, :, None, :], z.transpose(0, 1, 3, 2, 4), NEG_INF)
    return bias.reshape(rpb.shape[0], kh, GRID_W, kh * GRID_W)


def _na_kernel(q_ref, k_ref, v_ref, kc_ref, vc_ref, bias_ref, o_ref, *, rb, rows, scale):
    blk = pl.program_id(2)
    kc = kc_ref[...]
    vc = vc_ref[...]
    win = NA_WIN_H * GRID_W
    starts, s_lat, s_ctx = [], [], []
    for j in range(rb):
        r = blk * rb + j
        r0 = jnp.clip(r - NA_WIN_H // 2, 0, rows - NA_WIN_H)
        start = pl.multiple_of(r0 * GRID_W, GRID_W)
        starts.append(start)
        q = q_ref[j * GRID_W:(j + 1) * GRID_W, :]
        s_lat.append(_dot_nt(q, k_ref[pl.ds(start, win), :]) * scale + bias_ref[0, r - r0])
        s_ctx.append(_dot_nt(q, kc) * scale)
    p_lat, p_ctx, dens = [], [], []
    for j in range(rb):
        m = jnp.maximum(jnp.max(s_lat[j], axis=-1, keepdims=True), jnp.max(s_ctx[j], axis=-1, keepdims=True))
        p_l = jnp.exp(s_lat[j] - m)
        p_c = jnp.exp(s_ctx[j] - m)
        dens.append(jnp.sum(p_l, axis=-1, keepdims=True) + jnp.sum(p_c, axis=-1, keepdims=True))
        p_lat.append(p_l.astype(BF16))
        p_ctx.append(p_c.astype(BF16))
    for j in range(rb):
        o = _dot(p_lat[j], v_ref[pl.ds(starts[j], win), :]) + _dot(p_ctx[j], vc)
        o_ref[j * GRID_W:(j + 1) * GRID_W, :] = (o / dens[j]).astype(o_ref.dtype)


def _na_attention(p, bias_tab, *, off, batch, seq, ctx):
    rows = seq // GRID_W
    assert rows >= NA_WIN_H
    rb = 16 if rows % 16 == 0 else 1
    dh = NA_HEAD_DIM
    nblk = rows // rb
    qc, kc_, vc_ = off['qa'] // dh, off['ka'] // dh, off['va'] // dh
    ctx_blk0 = batch * seq // ctx
    win = NA_WIN_H * GRID_W
    return pl.pallas_call(
        functools.partial(_na_kernel, rb=rb, rows=rows, scale=dh ** -0.5),
        name="na_attention",
        grid=(batch, NA_HEADS, nblk),
        in_specs=[pl.BlockSpec((rb * GRID_W, dh), lambda b, h, r: (b * nblk + r, qc + h)),
                  pl.BlockSpec((seq, dh), lambda b, h, r: (b, kc_ + h)),
                  pl.BlockSpec((seq, dh), lambda b, h, r: (b, vc_ + h)),
                  pl.BlockSpec((ctx, dh), lambda b, h, r: (ctx_blk0 + b, kc_ + h)),
                  pl.BlockSpec((ctx, dh), lambda b, h, r: (ctx_blk0 + b, vc_ + h)),
                  pl.BlockSpec((1, NA_WIN_H, GRID_W, win), lambda b, h, r: (h, 0, 0, 0))],
        out_specs=pl.BlockSpec((rb * GRID_W, dh), lambda b, h, r: (b * nblk + r, h)),
        out_shape=jax.ShapeDtypeStruct((batch * seq, NA_HEADS * dh), BF16),
        compiler_params=_params(("parallel", "parallel", "arbitrary")),
    )(p, p, p, p, p, bias_tab)


def _ctx_na_kernel(q_ref, k_ref, v_ref, o_ref, *, scale):
    s = _dot_nt(q_ref[...], k_ref[...]) * scale
    m = jnp.max(s, axis=-1, keepdims=True)
    e = jnp.exp(s - m)
    den = jnp.sum(e, axis=-1, keepdims=True)
    o_ref[...] = (_dot(e.astype(BF16), v_ref[...]) / den).astype(o_ref.dtype)


def _ctx_na_attention(p, *, off, batch, seq, ctx):
    dh = NA_HEAD_DIM
    qc, kc_, vc_ = off['qa'] // dh, off['ka'] // dh, off['va'] // dh
    blk0 = batch * seq // ctx
    return pl.pallas_call(
        functools.partial(_ctx_na_kernel, scale=dh ** -0.5),
        name="ctx_na_attention",
        grid=(batch, NA_HEADS),
        in_specs=[pl.BlockSpec((ctx, dh), lambda b, h: (blk0 + b, qc + h)),
                  pl.BlockSpec((ctx, dh), lambda b, h: (blk0 + b, kc_ + h)),
                  pl.BlockSpec((ctx, dh), lambda b, h: (blk0 + b, vc_ + h))],
        out_specs=pl.BlockSpec((ctx, dh), lambda b, h: (b, h)),
        out_shape=jax.ShapeDtypeStruct((batch * ctx, NA_HEADS * dh), BF16),
        compiler_params=_params(("parallel", "parallel")),
    )(p, p, p)


def _split_maps(q):
    lane = lax.broadcasted_iota(jnp.int32, q.shape, 1)
    zero = jnp.zeros_like(q)
    return jnp.concatenate([jnp.where(lane < DIFF_QK_DIM, q, zero), jnp.where(lane >= DIFF_QK_DIM, q, zero)],
                           axis=0)


def _diff_finish(o1, o2, par_ref):
    lam = par_ref[1:2, :]
    o = o1 - lam * o2
    y = o * lax.rsqrt(jnp.mean(o * o, axis=-1, keepdims=True) + SUBLN_EPS)
    return y * par_ref[0:1, :]


ONES_ROWS = 16


def _diff_kernel(q_ref, k_ref, v_ref, kc_ref, vc_ref, par_ref, o_ref, vt_ref, vct_ref, s_ref, *, tk):
    tq = q_ref.shape[0]
    dv = v_ref.shape[1]
    n_chunks = k_ref.shape[0] // tk

    @pl.when(pl.program_id(2) == 0)
    def _():
        for c in range(n_chunks):
            vt_ref[c, 0:dv, :] = v_ref[c * tk:(c + 1) * tk, :].astype(F32).T.astype(BF16)
            vt_ref[c, dv:, :] = jnp.ones((ONES_ROWS, tk), BF16)
        vct_ref[0:dv, :] = vc_ref[...].astype(F32).T.astype(BF16)
        vct_ref[dv:, :] = jnp.ones((ONES_ROWS, vct_ref.shape[1]), BF16)

    q_t = q_ref[...].astype(F32).T
    row = lax.broadcasted_iota(jnp.int32, q_t.shape, 0)
    qm = jnp.concatenate([jnp.where(row < DIFF_QK_DIM, q_t, 0.0), jnp.where(row >= DIFF_QK_DIM, q_t, 0.0)],
                         axis=1).astype(BF16)

    def scores(c):
        start = pl.multiple_of(c * tk, tk)
        return _dot(k_ref[pl.ds(start, tk), :], qm)

    def consume(s, vtb, carry):
        m, acc = carry
        m_new = jnp.maximum(m, jnp.max(s, axis=0, keepdims=True))
        alpha = jnp.exp2(m - m_new)
        e = jnp.exp2(s - m_new).astype(BF16)
        return m_new, alpha * acc + _dot(vtb, e)

    per_trip = 8 if n_chunks % 8 == 0 else 2

    def body(j, carry):
        c0 = per_trip * j
        for t in range(per_trip):
            s_ref[(t + 1) % 2] = scores(jnp.minimum(c0 + t + 1, n_chunks - 1))
            carry = consume(s_ref[t % 2], vt_ref[c0 + t], carry)
        return carry

    init = (jnp.full((1, 2 * tq), -jnp.inf, F32), jnp.zeros((dv + ONES_ROWS, 2 * tq), F32))
    s_ref[0] = scores(0)
    carry = lax.fori_loop(0, n_chunks // per_trip, body, init)
    m, acc = consume(_dot(kc_ref[...], qm), vct_ref[...], carry)
    o = acc[0:dv] / acc[dv:dv + 1]
    o_ref[...] = _diff_finish(o[:, :tq].T, o[:, tq:].T, par_ref).astype(o_ref.dtype)


def _diff_attention(qr, kr, p, par, *, off, batch, seq, ctx):
    dq2 = 2 * DIFF_QK_DIM
    dv = DIFF_V_DIM
    assert dq2 == LANES and dv == LANES
    tq = _tile(seq, 512)
    tk = _tile(seq, 512)
    assert (seq // tk) % 2 == 0
    nq = seq // tq
    kdc, vdc = off['kd'] // dq2, off['vd'] // dv
    ctx_blk0 = batch * seq // ctx
    return pl.pallas_call(
        functools.partial(_diff_kernel, tk=tk),
        name="diff_attention",
        scratch_shapes=[pltpu.VMEM((seq // tk, dv + ONES_ROWS, tk), BF16), pltpu.VMEM((dv + ONES_ROWS, ctx), BF16),
                        pltpu.VMEM((2, tk, 2 * tq), F32)],
        grid=(batch, DIFF_HEADS, nq),
        in_specs=[pl.BlockSpec((tq, dq2), lambda b, h, i: (b * nq + i, h)),
                  pl.BlockSpec((seq, dq2), lambda b, h, i: (b, h)),
                  pl.BlockSpec((seq, dv), lambda b, h, i: (b, vdc + h)),
                  pl.BlockSpec((ctx, dq2), lambda b, h, i: (ctx_blk0 + b, kdc + h)),
                  pl.BlockSpec((ctx, dv), lambda b, h, i: (ctx_blk0 + b, vdc + h)),
                  pl.BlockSpec((8, dv), lambda b, h, i: (0, 0))],
        out_specs=pl.BlockSpec((tq, dv), lambda b, h, i: (b * nq + i, h)),
        out_shape=jax.ShapeDtypeStruct((batch * seq, DIFF_HEADS * dv), BF16),
        compiler_params=_params(("parallel", "parallel", "arbitrary")),
    )(qr, kr, p, p, p, par)


def _ctx_diff_kernel(q_ref, k_ref, v_ref, par_ref, o_ref, *, scale):
    n = q_ref.shape[0]
    s = _dot_nt(_split_maps(q_ref[...]), k_ref[...]) * scale
    m = jnp.max(s, axis=-1, keepdims=True)
    e = jnp.exp(s - m)
    den = jnp.sum(e, axis=-1, keepdims=True)
    o = _dot(e.astype(BF16), v_ref[...]) / den
    o_ref[...] = _diff_finish(o[:n], o[n:], par_ref).astype(o_ref.dtype)


def _ctx_diff_attention(p, par, *, off, batch, seq, ctx):
    dq2 = 2 * DIFF_QK_DIM
    dv = DIFF_V_DIM
    qdc, kdc, vdc = off['qd'] // dq2, off['kd'] // dq2, off['vd'] // dv
    blk0 = batch * seq // ctx
    return pl.pallas_call(
        functools.partial(_ctx_diff_kernel, scale=DIFF_QK_DIM ** -0.5),
        name="ctx_diff_attention",
        grid=(batch, DIFF_HEADS),
        in_specs=[pl.BlockSpec((ctx, dq2), lambda b, h: (blk0 + b, qdc + h)),
                  pl.BlockSpec((ctx, dq2), lambda b, h: (blk0 + b, kdc + h)),
                  pl.BlockSpec((ctx, dv), lambda b, h: (blk0 + b, vdc + h)),
                  pl.BlockSpec((8, dv), lambda b, h: (0, 0))],
        out_specs=pl.BlockSpec((ctx, dv), lambda b, h: (b, h)),
        out_shape=jax.ShapeDtypeStruct((batch * ctx, DIFF_HEADS * dv), BF16),
        compiler_params=_params(("parallel", "parallel")),
    )(p, p, p, par)


def _ret_tables(log_gamma, reverse):
    cs = RET_CHUNK
    k_scale = RET_QK_DIM ** -0.5
    pos = jnp.arange(cs, dtype=F32)
    lg = log_gamma[:, None]
    rel = pos[:, None] - pos[None, :]
    if reverse:
        rel = -rel
        q_decay = jnp.exp(lg * (cs - pos))
        k_decay = jnp.exp(lg * pos)
    else:
        q_decay = jnp.exp(lg * (pos + 1.0))
        k_decay = jnp.exp(lg * (cs - 1.0 - pos))
    intra = jnp.where(rel >= 0, jnp.exp(lg[:, :, None] * jnp.maximum(rel, 0.0)), 0.0) * k_scale
    chunk_decay = jnp.broadcast_to(jnp.exp(lg * cs)[:, :, None], (RET_HEADS, 1, RET_V_DIM))
    return intra, q_decay[:, :, None], (k_decay * k_scale)[:, :, None], chunk_decay


def _ret_kernel(*refs, nh, final, sub_order):
    q_refs, k_refs, v_refs = refs[0:nh], refs[nh:2 * nh], refs[2 * nh:3 * nh]
    pos = 3 * nh
    if final:
        g_refs = refs[pos:pos + nh]
        of_ref, gain_ref = refs[pos + nh:pos + nh + 2]
        pos += nh + 2
    intra_ref, qdec_ref, kdec_ref, cd_ref, o_ref, s_ref = refs[pos:pos + 6]
    dv = RET_V_DIM

    @pl.when(pl.program_id(1) == 0)
    def _():
        s_ref[...] = jnp.zeros_like(s_ref)

    cs = RET_CHUNK
    for sub in sub_order:
        rows = slice(sub * cs, (sub + 1) * cs)
        atts, outs = [], []
        for h in range(nh):
            atts.append((_dot_nt(q_refs[h][rows, :], k_refs[h][rows, :]) * intra_ref[h]).astype(BF16))
        for h in range(nh):
            qd = (q_refs[h][rows, :].astype(F32) * qdec_ref[h]).astype(BF16)
            outs.append(_dot(atts[h], v_refs[h][rows, :]) + _dot(qd, s_ref[h].astype(BF16)))
        for h in range(nh):
            kd_t = (k_refs[h][rows, :].astype(F32) * kdec_ref[h]).T.astype(BF16)
            s_ref[h] = s_ref[h] * cd_ref[h] + _dot(kd_t, v_refs[h][rows, :])
        for h in range(nh):
            o = outs[h]
            cols = slice(h * dv, (h + 1) * dv)
            if final:
                o = o + of_ref[rows, cols]
                mu = jnp.mean(o, axis=-1, keepdims=True)
                var = jnp.mean(jnp.square(o - mu), axis=-1, keepdims=True)
                y = (o - mu) * lax.rsqrt(var + NORM_EPS) * gain_ref[:, cols]
                o_ref[rows, cols] = (_silu(g_refs[h][rows, :].astype(F32)) * y).astype(o_ref.dtype)
            else:
                o_ref[rows, cols] = o


def _retention(p, tabs_f, tabs_b, gain, *, off, batch, seq, ctx, n_tok):
    cs, nh, dk, dv = RET_CHUNK, RET_HEADS, RET_QK_DIM, RET_V_DIM
    per = 2 if (ctx // cs) % 2 == 0 and (seq // cs) % 2 == 0 else 1
    blk = per * cs
    lc, sc = ctx // blk, seq // blk
    ctx_blk0 = batch * seq // blk
    vw = nh * dv
    assert off['qb'] % dk == 0 and off['kb'] % dk == 0 and off['vb'] % dv == 0 and off['gb'] % dv == 0

    def chunk_fwd(b, t):
        return jnp.where(t < lc, ctx_blk0 + b * lc + t, b * sc + (t - lc))

    def chunk_bwd(b, t):
        return jnp.where(t < lc, ctx_blk0 + b * lc + (lc - 1 - t), b * sc + (sc - 1 - (t - lc)))

    def run(chunk, tabs, final, of):
        def head_spec(width, col0, h):
            return pl.BlockSpec((blk, width), lambda b, t: (chunk(b, t), col0 + h))

        in_specs = ([head_spec(dk, off['qb'] // dk, h) for h in range(nh)]
                    + [head_spec(dk, off['kb'] // dk, h) for h in range(nh)]
                    + [head_spec(dv, off['vb'] // dv, h) for h in range(nh)])
        args = [p] * (3 * nh)
        if final:
            in_specs += [head_spec(dv, off['gb'] // dv, h) for h in range(nh)]
            in_specs += [pl.BlockSpec((blk, vw), lambda b, t: (chunk(b, t), 0)),
                         pl.BlockSpec((1, vw), lambda b, t: (0, 0))]
            args += [p] * nh + [of, gain.reshape(1, vw).astype(F32)]
        in_specs += [pl.BlockSpec((nh, cs, cs), lambda b, t: (0, 0, 0)),
                     pl.BlockSpec((nh, cs, 1), lambda b, t: (0, 0, 0)),
                     pl.BlockSpec((nh, cs, 1), lambda b, t: (0, 0, 0)),
                     pl.BlockSpec((nh, 1, dv), lambda b, t: (0, 0, 0))]
        args += list(tabs)
        return pl.pallas_call(
            functools.partial(_ret_kernel, nh=nh, final=final,
                              sub_order=tuple(reversed(range(per))) if final else tuple(range(per))),
            name="retention_bwd_norm_gate" if final else "retention_fwd",
            grid=(batch, lc + sc),
            in_specs=in_specs,
            out_specs=pl.BlockSpec((blk, vw), lambda b, t: (chunk(b, t), 0)),
            out_shape=jax.ShapeDtypeStruct((n_tok, vw), BF16 if final else F32),
            scratch_shapes=[pltpu.VMEM((nh, dk, dv), F32)],
            compiler_params=_params(("parallel", "arbitrary")),
        )(*args)

    o_f = run(chunk_fwd, tabs_f, False, None)
    return run(chunk_bwd, tabs_b, True, o_f)


def _merge_kernel(*refs, lat_tiles):
    oa_ref, ob_ref, od_ref = refs[0:3]
    oa = oa_ref[...]
    od = od_ref[...]
    pos = 3
    if lat_tiles is not None:
        is_ctx = pl.program_id(0) >= lat_tiles
        oa = jnp.where(is_ctx, refs[3][...], oa)
        od = jnp.where(is_ctx, refs[4][...], od)
        pos = 5
    ga_ref, gb_ref, gd_ref, wa_ref, wb_ref, wd_ref, o_ref = refs[pos:pos + 7]
    j = pl.program_id(1)
    y = jax.nn.sigmoid(ga_ref[...].astype(F32)) * _dot(oa, wa_ref[j])
    y += jax.nn.sigmoid(gb_ref[...].astype(F32)) * _dot(ob_ref[...], wb_ref[j])
    y += jax.nn.sigmoid(gd_ref[...].astype(F32)) * _dot(od, wd_ref[j])
    o_ref[...] = y.astype(o_ref.dtype)


def _col_blocked(w, tn):
    k, n = w.shape
    return w.reshape(k, n // tn, tn).transpose(1, 0, 2)


def _merge(oa, ob, od, ctx_pair, p, wa, wb, wd, *, off, n_rows, n_lat, tm):
    d = D_MODEL
    tn = _tile(d, 512)
    assert off['gates'] % tn == 0
    g0 = off['gates'] // tn
    nj = d // tn
    lat_tiles = n_lat // tm

    def lat(a):
        return pl.BlockSpec((tm, a.shape[1]), lambda i, j: (jnp.minimum(i, lat_tiles - 1), 0))

    def ctx_rows(a):
        return pl.BlockSpec((tm, a.shape[1]), lambda i, j: (jnp.maximum(i - lat_tiles, 0), 0))

    def wspec(w):
        return pl.BlockSpec((nj, w.shape[0], tn), lambda i, j: (0, 0, 0))

    def gspec(br):
        return pl.BlockSpec((tm, tn), lambda i, j: (i, g0 + br * nj + j))

    in_specs = [lat(oa), pl.BlockSpec((tm, ob.shape[1]), lambda i, j: (i, 0)), lat(od)]
    args = [oa, ob, od]
    if ctx_pair is not None:
        in_specs += [ctx_rows(ctx_pair[0]), ctx_rows(ctx_pair[1])]
        args += list(ctx_pair)
    in_specs += [gspec(0), gspec(1), gspec(2), wspec(wa), wspec(wb), wspec(wd)]
    args += [p, p, p, _col_blocked(wa, tn), _col_blocked(wb, tn), _col_blocked(wd, tn)]
    return pl.pallas_call(
        functools.partial(_merge_kernel, lat_tiles=lat_tiles if ctx_pair is not None else None),
        name="branch_merge",
        grid=(n_rows // tm, nj),
        in_specs=in_specs,
        out_specs=pl.BlockSpec((tm, tn), lambda i, j: (i, j)),
        out_shape=jax.ShapeDtypeStruct((n_rows, d), BF16),
        compiler_params=_params(("parallel", "arbitrary")),
    )(*args)


def _router_kernel(h_ref, mod_ref, g_ref, rw_ref, rb_ref, u_ref, idx_ref, wt_ref, *, row):
    mod = mod_ref[0]
    u = _normmod(h_ref[...], g_ref[...], mod[row:row + 1], mod[row + 1:row + 2])
    u_hi = u.astype(BF16)
    u_ref[...] = u_hi
    u_lo = (u - u_hi.astype(F32)).astype(BF16)
    w = rw_ref[...]
    w_hi = w.astype(BF16)
    w_lo = (w - w_hi.astype(F32)).astype(BF16)
    logits = _dot(u_hi, w_hi) + _dot(u_hi, w_lo) + _dot(u_lo, w_hi) + rb_ref[...]
    lane = lax.broadcasted_iota(jnp.int32, logits.shape, 1)
    lg = jnp.where(lane < N_EXPERTS, logits, -jnp.inf)
    v1 = jnp.max(lg, axis=-1, keepdims=True)
    lane_f = lane.astype(F32)
    i1 = jnp.min(jnp.where(lg == v1, lane_f, float(LANES)), axis=-1, keepdims=True).astype(jnp.int32)
    lg2 = jnp.where(lane == i1, -jnp.inf, lg)
    v2 = jnp.max(lg2, axis=-1, keepdims=True)
    i2 = jnp.min(jnp.where(lg2 == v2, lane_f, float(LANES)), axis=-1, keepdims=True).astype(jnp.int32)
    e = jnp.exp(v2 - v1)
    w1 = 1.0 / (1.0 + e)
    w2 = e / (1.0 + e)
    idx_ref[...] = jnp.where(lane == 0, i1, jnp.where(lane == 1, i2, 0))
    wt_ref[...] = jnp.where(lane == 0, w1, jnp.where(lane == 1, w2, 0.0))


def _router(h, mod, g, router_w, router_b, *, row, n_rows, tm, seq, batch):
    d = D_MODEL
    rw = jnp.zeros((d, LANES), F32).at[:, :N_EXPERTS].set(router_w.astype(F32))
    rb = jnp.zeros((1, LANES), F32).at[0, :N_EXPERTS].set(router_b.astype(F32))
    mod_idx = _mod_index(tm, seq, batch)
    return pl.pallas_call(
        functools.partial(_router_kernel, row=row),
        name="norm_router",
        grid=(n_rows // tm,),
        in_specs=[pl.BlockSpec((tm, d), lambda i: (i, 0)),
                  pl.BlockSpec((1, 8, d), lambda i: mod_idx(i, 0)),
                  pl.BlockSpec((1, d), lambda i: (0, 0)),
                  pl.BlockSpec((d, LANES), lambda i: (0, 0)),
                  pl.BlockSpec((1, LANES), lambda i: (0, 0))],
        out_specs=[pl.BlockSpec((tm, d), lambda i: (i, 0)),
                   pl.BlockSpec((tm, LANES), lambda i: (i, 0)),
                   pl.BlockSpec((tm, LANES), lambda i: (i, 0))],
        out_shape=[jax.ShapeDtypeStruct((n_rows, d), BF16),
                   jax.ShapeDtypeStruct((n_rows, LANES), jnp.int32),
                   jax.ShapeDtypeStruct((n_rows, LANES), F32)],
        compiler_params=_params(("parallel",)),
    )(h, mod, g.reshape(1, d), rw, rb)


def _route_plan(ridx, rwt, tme):
    n_tok = ridx.shape[0]
    n2 = n_tok * TOP_K
    n_pad = n2 + N_EXPERTS * tme
    n_tiles = n_pad // tme
    e_flat = ridx.reshape(-1)
    order = jnp.argsort(e_flat, stable=True).astype(jnp.int32)
    counts = jnp.sum(e_flat[:, None] == jnp.arange(N_EXPERTS)[None, :], axis=0).astype(jnp.int32)
    padded = ((counts + tme - 1) // tme) * tme
    pend = jnp.cumsum(padded)
    pstart = pend - padded
    ustart = jnp.cumsum(counts) - counts
    tile_start = jnp.arange(n_tiles, dtype=jnp.int32) * tme
    tile_expert = jnp.minimum(jnp.sum(tile_start[:, None] >= pend[None, :], axis=1), N_EXPERTS - 1).astype(jnp.int32)
    tile_valid = (tile_start < pend[-1]).astype(jnp.int32)
    slot = jnp.arange(n_pad, dtype=jnp.int32)
    slot_e = jnp.repeat(tile_expert, tme)
    within = slot - pstart[slot_e]
    valid = (within < counts[slot_e]) & (jnp.repeat(tile_valid, tme) > 0)
    src = jnp.clip(ustart[slot_e] + within, 0, n2 - 1)
    assign = order[src]
    perm_tok = jnp.where(valid, assign // TOP_K, 0)
    w_sorted = jnp.where(valid, rwt.reshape(-1)[assign], 0.0)
    e_sorted = e_flat[order]
    pos = jnp.arange(n2, dtype=jnp.int32) - ustart[e_sorted] + pstart[e_sorted]
    slot_of = pos[jnp.argsort(order)].reshape(n_tok, TOP_K)
    return perm_tok, w_sorted, slot_of, tile_expert, tile_valid


CAST_SLICES = 4


def _moe_up_kernel(te_ref, tv_ref, tf_ref, a_ref, w1_ref, w3_ref, w2_ref, o_ref, w2b_ref, w1b_ref, w3b_ref):
    del te_ref
    i = pl.program_id(1)
    first = tf_ref[i] > 0
    valid = tv_ref[i] > 0
    w2b_ref[...] = w2_ref[...].astype(BF16)

    @pl.when(first)
    def _():
        a = a_ref[...]
        tn = o_ref.shape[1]
        cw = tn // CAST_SLICES
        for c in range(CAST_SLICES):
            cols = slice(c * cw, (c + 1) * cw)
            w1c = w1_ref[0, :, cols].astype(BF16)
            w3c = w3_ref[0, :, cols].astype(BF16)
            w1b_ref[:, cols] = w1c
            w3b_ref[:, cols] = w3c
            y = _silu(_dot(a, w1c)) * _dot(a, w3c)
            o_ref[:, cols] = jnp.where(valid, y, 0.0).astype(o_ref.dtype)

    @pl.when(jnp.logical_and(jnp.logical_not(first), valid))
    def _():
        a = a_ref[...]
        o_ref[...] = (_silu(_dot(a, w1b_ref[...])) * _dot(a, w3b_ref[...])).astype(o_ref.dtype)

    @pl.when(jnp.logical_and(jnp.logical_not(first), jnp.logical_not(valid)))
    def _():
        o_ref[...] = jnp.zeros_like(o_ref)


def _moe_up(a, w1, w3, w2, tile_expert, tile_valid, *, tme):
    n_pad, d = a.shape
    n_exp, f, d_out = w2.shape
    tn = _tile(f, 1024)
    n_tiles = n_pad // tme
    steps = (f // tn) * n_tiles
    ck = next(c for c in range(16, f + 1, 16) if f % c == 0 and n_exp * (f // c) <= steps)
    per_exp = f // ck
    last = n_exp * per_exp - 1

    def w2_index(j, i, te, tv, tf):
        s = jnp.minimum(j * n_tiles + i, last)
        return (s // per_exp, s % per_exp, 0)

    tile_first = jnp.concatenate([jnp.ones((1,), jnp.int32),
                                  (tile_expert[1:] != tile_expert[:-1]).astype(jnp.int32)])
    return pl.pallas_call(
        _moe_up_kernel,
        name="moe_up",
        grid_spec=pltpu.PrefetchScalarGridSpec(
            num_scalar_prefetch=3,
            grid=(f // tn, n_tiles),
            in_specs=[pl.BlockSpec((tme, d), lambda j, i, te, tv, tf: (i, 0)),
                      pl.BlockSpec((1, d, tn), lambda j, i, te, tv, tf: (te[i], 0, j)),
                      pl.BlockSpec((1, d, tn), lambda j, i, te, tv, tf: (te[i], 0, j)),
                      pl.BlockSpec((1, ck, d_out), w2_index)],
            out_specs=[pl.BlockSpec((tme, tn), lambda j, i, te, tv, tf: (i, j)),
                       pl.BlockSpec((1, ck, d_out), w2_index)],
            scratch_shapes=[pltpu.VMEM((d, tn), BF16), pltpu.VMEM((d, tn), BF16)]),
        out_shape=[jax.ShapeDtypeStruct((n_pad, f), BF16), jax.ShapeDtypeStruct(w2.shape, BF16)],
        compiler_params=_params(("arbitrary", "arbitrary")),
    )(tile_expert, tile_valid, tile_first, a, w1, w3, w2)


def _moe_down_kernel(te_ref, tv_ref, a_ref, w_ref, ws_ref, o_ref):
    del te_ref
    i = pl.program_id(1)

    @pl.when(tv_ref[i] > 0)
    def _():
        o_ref[...] = (ws_ref[...] * _dot(a_ref[...], w_ref[0])).astype(o_ref.dtype)

    @pl.when(tv_ref[i] == 0)
    def _():
        o_ref[...] = jnp.zeros_like(o_ref)


def _moe_down(a, w2, w_sorted, tile_expert, tile_valid, *, tme):
    n_pad, f = a.shape
    d = w2.shape[2]
    tn = _tile(d, 1024)
    return pl.pallas_call(
        _moe_down_kernel,
        name="moe_down",
        grid_spec=pltpu.PrefetchScalarGridSpec(
            num_scalar_prefetch=2,
            grid=(d // tn, n_pad // tme),
            in_specs=[pl.BlockSpec((tme, f), lambda j, i, te, tv: (i, 0)),
                      pl.BlockSpec((1, f, tn), lambda j, i, te, tv: (te[i], 0, j)),
                      pl.BlockSpec((tme, 1), lambda j, i, te, tv: (i, 0))],
            out_specs=pl.BlockSpec((tme, tn), lambda j, i, te, tv: (i, j))),
        out_shape=jax.ShapeDtypeStruct((n_pad, d), BF16),
        compiler_params=_params(("parallel", "arbitrary")),
    )(tile_expert, tile_valid, a, w2, w_sorted.reshape(n_pad, 1))


def _combine_norm_kernel(h_ref, y1_ref, y2_ref, mod_ref, g_ref, o_ref, *, row):
    gate = mod_ref[0][row:row + 1]
    x = h_ref[...] + gate * (y1_ref[...].astype(F32) + y2_ref[...].astype(F32))
    ms = jnp.mean(x * x, axis=-1, keepdims=True)
    o_ref[...] = x * lax.rsqrt(ms + NORM_EPS) * g_ref[...]


def _combine_norm(h, y12, mod, g, *, row, n_rows, tm, seq, batch):
    d = D_MODEL
    mod_idx = _mod_index(tm, seq, batch)
    rowspec = pl.BlockSpec((tm, d), lambda i: (i, 0))
    nt = n_rows // tm
    return pl.pallas_call(
        functools.partial(_combine_norm_kernel, row=row),
        name="moe_combine_final_norm",
        grid=(nt,),
        in_specs=[rowspec, rowspec, pl.BlockSpec((tm, d), lambda i: (nt + i, 0)),
                  pl.BlockSpec((1, 8, d), lambda i: mod_idx(i, 0)),
                  pl.BlockSpec((1, d), lambda i: (0, 0))],
        out_specs=rowspec,
        out_shape=jax.ShapeDtypeStruct((n_rows, d), F32),
        compiler_params=_params(("parallel",)),
    )(h, y12, y12, mod, g.reshape(1, d))


def _token_mixers(h, mod, lp, layer_idx, rope_tabs, *, dims, need_ctx):
    batch, seq, ctx, n_tok, n_lat, tm = dims
    off, in_w = _layout()
    bf = lambda a: a.astype(BF16)
    p = _normmod_mm(h, mod, lp['norm1'], bf(lp['w_in']), row=0, n_rows=n_tok, tm=WIDE_ROW_FACTOR * tm, seq=seq,
                    batch=batch, tn_pref=768)

    bias_tab = _na_bias_table(lp['na_rpb'], seq // GRID_W)
    oa = _na_attention(p, bias_tab, off=off, batch=batch, seq=seq, ctx=ctx)

    log_gamma = jax.nn.log_sigmoid(lp['ret_decay'].astype(F32))
    ob = _retention(p, _ret_tables(log_gamma[0], False), _ret_tables(log_gamma[1], True), lp['ret_gn'],
                    off=off, batch=batch, seq=seq, ctx=ctx, n_tok=n_tok)

    lam_init = 0.8 - 0.6 * math.exp(-0.3 * layer_idx)
    lq1, lk1, lq2, lk2 = lp['diff_lam'].astype(F32)
    lam = jnp.exp(jnp.sum(lq1 * lk1)) - jnp.exp(jnp.sum(lq2 * lk2)) + lam_init
    par = jnp.zeros((8, DIFF_V_DIM), F32)
    par = par.at[0].set(lp['diff_subln'].astype(F32) * (1.0 - lam_init)).at[1].set(lam)
    qr, kr = _rope(p, rope_tabs, off=off, n_lat=n_lat, seq=seq, tm=tm)
    od = _diff_attention(qr, kr, p, par, off=off, batch=batch, seq=seq, ctx=ctx)

    ctx_pair = None
    if need_ctx:
        ctx_pair = (_ctx_na_attention(p, off=off, batch=batch, seq=seq, ctx=ctx),
                    _ctx_diff_attention(p, par, off=off, batch=batch, seq=seq, ctx=ctx))
    n_rows = n_tok if need_ctx else n_lat
    ymid = _merge(oa, ob, od, ctx_pair, p, bf(lp['w_br_a']), bf(lp['w_br_b']), bf(lp['w_br_c']), off=off,
                  n_rows=n_rows, n_lat=n_lat, tm=tm)
    return _mm_res(ymid, bf(lp['w_out']), h, mod, row=2, n_rows=n_rows, tm=tm, seq=seq, batch=batch,
                   weight_tile_bytes=8 << 20)


def _forward(x, c, ctx_tok, c_ctx, layers, final_norm):
    batch, seq, d = x.shape
    ctx = ctx_tok.shape[1]
    n_lat = batch * seq
    n_tok = n_lat + batch * ctx
    tm = min(ROW_TILE, batch * ctx)
    assert d == D_MODEL and seq % tm == 0 and (batch * ctx) % tm == 0 and seq % GRID_W == 0
    assert seq % RET_CHUNK == 0 and ctx % RET_CHUNK == 0 and n_lat % ctx == 0 and batch + 1 <= 8
    dims = (batch, seq, ctx, n_tok, n_lat, tm)
    bf = lambda a: a.astype(BF16)

    h = jnp.concatenate([x.reshape(n_lat, d), ctx_tok.reshape(batch * ctx, d)], axis=0).astype(F32)
    cond8 = jnp.zeros((8, d), F32).at[:batch].set(c).at[batch].set(c_ctx)
    rope_tabs = _rope_tables(seq)
    n_layers = len(layers)
    out = None
    for li, lp in enumerate(layers):
        need_ctx = li < n_layers - 1
        m = _adaln(cond8, lp['w_ada'], lp['b_ada'])
        mod = jnp.zeros((batch + 1, 8, d), F32).at[:, :6].set(m[:batch + 1].reshape(batch + 1, 6, d))
        h = _token_mixers(h, mod, lp, li, rope_tabs, dims=dims, need_ctx=need_ctx)
        n_rows = n_tok if need_ctx else n_lat
        if 'ffn_w1' in lp:
            a = _normmod_swiglu(h, mod, lp['norm2'], bf(lp['ffn_w1']), bf(lp['ffn_w3']), row=3, n_rows=n_rows,
                                tm=WIDE_ROW_FACTOR * tm, seq=seq, batch=batch)
            h = _mm_res(a, bf(lp['ffn_w2']), h, mod, row=5, n_rows=n_rows, tm=WIDE_ROW_FACTOR * tm, seq=seq,
                        batch=batch, weight_tile_bytes=3 << 20)
        else:
            assert not need_ctx and li == n_layers - 1
            tme = min(MOE_ROW_TILE, n_lat)
            u, ridx, rwt = _router(h, mod, lp['norm2'], lp['router_w'], lp['router_b'], row=3, n_rows=n_lat,
                                   tm=tm, seq=seq, batch=batch)
            perm_tok, w_sorted, slot_of, tile_expert, tile_valid = _route_plan(ridx[:, :TOP_K], rwt[:, :TOP_K], tme)
            u_sorted = u.at[perm_tok].get(mode='promise_in_bounds')
            a, w2_bf16 = _moe_up(u_sorted, lp['exp_w1'], lp['exp_w3'], lp['exp_w2'], tile_expert, tile_valid, tme=tme)
            y = _moe_down(a, w2_bf16, w_sorted, tile_expert, tile_valid, tme=tme)
            y12 = y.at[slot_of.T.reshape(-1)].get(mode='promise_in_bounds')
            out = _combine_norm(h, y12, mod, final_norm, row=5, n_rows=n_lat, tm=tm, seq=seq, batch=batch)
    return out.reshape(batch, seq, d)


def kernel(x, c, ctx, c_ctx, l0_w_ada, l0_b_ada, l0_norm1, l0_w_in, l0_na_rpb, l0_ret_decay, l0_ret_gn, l0_diff_lam, l0_diff_subln, l0_w_br_a, l0_w_br_b, l0_w_br_c, l0_w_out, l0_norm2, l0_ffn_w1, l0_ffn_w3, l0_ffn_w2, l1_w_ada, l1_b_ada, l1_norm1, l1_w_in, l1_na_rpb, l1_ret_decay, l1_ret_gn, l1_diff_lam, l1_diff_subln, l1_w_br_a, l1_w_br_b, l1_w_br_c, l1_w_out, l1_norm2, l1_router_w, l1_router_b, l1_exp_w1, l1_exp_w3, l1_exp_w2, final_norm):
    layers = (
        dict(w_ada=l0_w_ada, b_ada=l0_b_ada, norm1=l0_norm1, w_in=l0_w_in, na_rpb=l0_na_rpb,
             ret_decay=l0_ret_decay, ret_gn=l0_ret_gn, diff_lam=l0_diff_lam, diff_subln=l0_diff_subln,
             w_br_a=l0_w_br_a, w_br_b=l0_w_br_b, w_br_c=l0_w_br_c, w_out=l0_w_out, norm2=l0_norm2,
             ffn_w1=l0_ffn_w1, ffn_w3=l0_ffn_w3, ffn_w2=l0_ffn_w2),
        dict(w_ada=l1_w_ada, b_ada=l1_b_ada, norm1=l1_norm1, w_in=l1_w_in, na_rpb=l1_na_rpb,
             ret_decay=l1_ret_decay, ret_gn=l1_ret_gn, diff_lam=l1_diff_lam, diff_subln=l1_diff_subln,
             w_br_a=l1_w_br_a, w_br_b=l1_w_br_b, w_br_c=l1_w_br_c, w_out=l1_w_out, norm2=l1_norm2,
             router_w=l1_router_w, router_b=l1_router_b, exp_w1=l1_exp_w1, exp_w3=l1_exp_w3,
             exp_w2=l1_exp_w2),
    )
    return _forward(x, c, ctx, c_ctx, layers, final_norm)
```
